```python
import math
import jax, jax.numpy as jnp
from jax import lax
import numpy as np

D_MODEL = 1024
BATCH = 4
SEQ = 4096
DEPTH = 4
DEC_BATCH = 128
DEC_SEQ = 8
PAST_LEN = 8192
PAGE_SIZE = 128

RW_HEADS = 8
RW_HD = 64
RW_W = RW_HEADS * RW_HD
W_LORA = 64
A_LORA = 64
G_LORA = 160
RW_COLS = 3 * RW_W + W_LORA + A_LORA + G_LORA
RW_GN_EPS = 64e-5
SB_HEADS = 8
SB_KV_HEADS = 2
SB_HD = 64
SB_COLS = (SB_HEADS + 2 * SB_KV_HEADS) * SB_HD
MLA_HEADS = 8
Q_LORA = 256
KV_LORA = 256
NOPE_DIM = 64
ROPE_DIM = 32
V_DIM = 64
MLA_COLS = Q_LORA + KV_LORA + ROPE_DIM
MLA_SCALE = (NOPE_DIM + ROPE_DIM) ** -0.5
ROPE_BASE = 10000.0
N_BRANCH = 3
GATE_COLS = N_BRANCH * D_MODEL
IN_COLS = RW_COLS + SB_COLS + MLA_COLS + GATE_COLS
D_FF = 2816
CONV_W = 3
Q_BLOCK = 128
NORM_EPS = 1e-6

kernel_name = 'hybrid_rwkv7_stickbreak_mla_convffn_step'


def _split_cols(x, sizes):
    idx = [int(i) for i in np.cumsum(sizes)[:-1]]
    return jnp.split(x, idx, axis=-1)


def rms_norm(x, g):
    xf = x.astype(jnp.float32)
    y = xf * lax.rsqrt(jnp.mean(xf * xf, axis=-1, keepdims=True) + NORM_EPS)
    return (y * g.astype(jnp.float32)).astype(x.dtype)


def apply_rope(x, pos):
    half = ROPE_DIM // 2
    inv = jnp.exp(-math.log(ROPE_BASE) * jnp.arange(half, dtype=jnp.float32) / half)
    ang = pos.astype(jnp.float32)[:, None] * inv[None, :]
    shape = (1, pos.shape[0]) + (1,) * (x.ndim - 3) + (half,)
    cos = jnp.cos(ang).reshape(shape)
    sin = jnp.sin(ang).reshape(shape)
    xf = x.astype(jnp.float32)
    x1, x2 = xf[..., :half], xf[..., half:]
    return jnp.concatenate([x1 * cos - x2 * sin, x2 * cos + x1 * sin], axis=-1).astype(x.dtype)


def sweep_query_blocks(attend, q_parts, qpos):
    T = qpos.shape[0]
    if T <= Q_BLOCK or T % Q_BLOCK:
        return attend(q_parts, qpos)
    nb = T // Q_BLOCK
    blocks = tuple(jnp.moveaxis(q.reshape((q.shape[0], nb, Q_BLOCK) + q.shape[2:]), 1, 0) for q in q_parts)
    out = lax.map(lambda blk: attend(blk[0], blk[1]), (blocks, qpos.reshape(nb, Q_BLOCK)))
    out = jnp.moveaxis(out, 0, 1)
    return out.reshape((out.shape[0], T) + out.shape[3:])


def stick_breaking_attend(q, k, v, qpos, kpos):
    z = jnp.einsum('btkgd,bskd->bkgts', q.astype(jnp.float32), k.astype(jnp.float32)) * (SB_HD ** -0.5)
    mask = kpos[None, :] < qpos[:, None]
    log_1m = jnp.where(mask, jax.nn.log_sigmoid(-z), 0.0)
    after = lax.cumsum(log_1m, axis=4, reverse=True) - log_1m
    a = jnp.where(mask, jnp.exp(jax.nn.log_sigmoid(z) + after), 0.0)
    o = jnp.einsum('bkgts,bskd->btkgd', a, v.astype(jnp.float32))
    return o.astype(q.dtype)


def mla_attend(q_lat, q_rope, ckv, krope, qpos, kpos):
    s = (jnp.einsum('bthc,bsc->bhts', q_lat.astype(jnp.float32), ckv.astype(jnp.float32))
         + jnp.einsum('bthr,bsr->bhts', q_rope.astype(jnp.float32), krope.astype(jnp.float32))) * MLA_SCALE
    mask = kpos[None, :] <= qpos[:, None]
    p = jax.nn.softmax(jnp.where(mask, s, -jnp.inf), axis=-1)
    o = jnp.einsum('bhts,bsc->bthc', p, ckv.astype(jnp.float32))
    return o.astype(q_lat.dtype)


def rwkv7_time_mix(c_rw, shift_prev, S0, mu, w0, w_up, a0, a_up, g_up, k_k, k_a, r_k, ln_w, ln_b):
    B, T, _ = c_rw.shape
    f32 = jnp.float32
    pf = c_rw.astype(f32)
    prev = jnp.concatenate([shift_prev.astype(f32)[:, None], pf[:, :-1]], axis=1)
    xs = pf + (prev - pf) * mu.astype(f32)
    r, k, v, wd, ad, gd = _split_cols(xs, (RW_W, RW_W, RW_W, W_LORA, A_LORA, G_LORA))
    w = -jax.nn.softplus(-(w0 + jnp.tanh(wd) @ w_up)) - 0.5
    decay = jnp.exp(-jnp.exp(w))
    a = jax.nn.sigmoid(a0 + ad @ a_up)
    g = jax.nn.sigmoid(gd) @ g_up
    hs = (B, T, RW_HEADS, RW_HD)
    r, k, v, decay, a = r.reshape(hs), k.reshape(hs), v.reshape(hs), decay.reshape(hs), a.reshape(hs)
    kk = k * k_k.reshape(RW_HEADS, RW_HD)
    kk = kk * lax.rsqrt(jnp.maximum(jnp.sum(kk * kk, axis=-1, keepdims=True), 1e-24))
    k = k * (1.0 + (a - 1.0) * k_a.reshape(RW_HEADS, RW_HD))

    def step(S, inp):
        r_t, w_t, k_t, v_t, kk_t, a_t = inp
        sa = jnp.einsum('bhvk,bhk->bhv', S, -kk_t)
        S = S * w_t[:, :, None, :] + sa[..., None] * (kk_t * a_t)[:, :, None, :] + v_t[..., None] * k_t[:, :, None, :]
        return S, jnp.einsum('bhvk,bhk->bhv', S, r_t)

    tm = lambda t: jnp.moveaxis(t, 1, 0)
    S_fin, o = lax.scan(step, S0.astype(f32), (tm(r), tm(decay), tm(k), tm(v), tm(kk), tm(a)))
    o = jnp.moveaxis(o, 0, 1)
    mean = jnp.mean(o, axis=-1, keepdims=True)
    var = jnp.mean(jnp.square(o - mean), axis=-1, keepdims=True)
    o = ((o - mean) * lax.rsqrt(var + RW_GN_EPS)).reshape(B, T, RW_W) * ln_w + ln_b
    bonus = (jnp.sum(r * k * r_k, axis=-1, keepdims=True) * v).reshape(B, T, RW_W)
    out = (o + bonus) * g
    return out.astype(c_rw.dtype), S_fin.astype(c_rw.dtype), c_rw[:, -1]


def mixer_sublayer(x, pos, rw_S0, rw_shift0, sb_k_past, sb_v_past, ckv_past, kr_past,
                   norm_g, w_in, rw_mu, rw_w0, rw_w_up, rw_a0, rw_a_up, rw_g_up, rw_k_k, rw_k_a, rw_r_k,
                   rw_ln_w, rw_ln_b, mla_q_norm_g, mla_kv_norm_g, mla_w_uq, mla_w_uk, mla_w_uv,
                   p_rwkv, p_sb, p_mla, w_out):
    B, T, _ = x.shape
    h = rms_norm(x, norm_g)
    proj = jnp.einsum('btd,dc->btc', h, w_in)
    c_rw, c_sb, c_mla, c_gate = _split_cols(proj, (RW_COLS, SB_COLS, MLA_COLS, GATE_COLS))

    o_rw, S_new, shift_new = rwkv7_time_mix(c_rw, rw_shift0, rw_S0, rw_mu, rw_w0, rw_w_up, rw_a0, rw_a_up,
                                            rw_g_up, rw_k_k, rw_k_a, rw_r_k, rw_ln_w, rw_ln_b)

    if sb_k_past is None:
        kpos = pos
    else:
        kpos = jnp.concatenate([jnp.arange(sb_k_past.shape[1], dtype=pos.dtype), pos])

    q_sb, k_sb, v_sb = _split_cols(c_sb, (SB_HEADS * SB_HD, SB_KV_HEADS * SB_HD, SB_KV_HEADS * SB_HD))
    q_sb = q_sb.reshape(B, T, SB_KV_HEADS, SB_HEADS // SB_KV_HEADS, SB_HD)
    k_sb = k_sb.reshape(B, T, SB_KV_HEADS, SB_HD)
    v_sb = v_sb.reshape(B, T, SB_KV_HEADS, SB_HD)
    k_all = k_sb if sb_k_past is None else jnp.concatenate([sb_k_past.astype(k_sb.dtype), k_sb], axis=1)
    v_all = v_sb if sb_v_past is None else jnp.concatenate([sb_v_past.astype(v_sb.dtype), v_sb], axis=1)
    o_sb = sweep_query_blocks(lambda qs, qp: stick_breaking_attend(qs[0], k_all, v_all, qp, kpos), (q_sb,), pos)
    o_sb = o_sb.reshape(B, T, SB_HEADS * SB_HD)

    c_q, c_kv, k_r = _split_cols(c_mla, (Q_LORA, KV_LORA, ROPE_DIM))
    c_q = rms_norm(c_q, mla_q_norm_g)
    c_kv = rms_norm(c_kv, mla_kv_norm_g)
    k_r = apply_rope(k_r, pos)
    q = jnp.einsum('btc,chd->bthd', c_q, mla_w_uq)
    q_nope, q_rope = q[..., :NOPE_DIM], apply_rope(q[..., NOPE_DIM:], pos)
    q_lat = jnp.einsum('bthn,chn->bthc', q_nope, mla_w_uk)
    ckv_all = c_kv if ckv_past is None else jnp.concatenate([ckv_past.astype(c_kv.dtype), c_kv], axis=1)
    kr_all = k_r if kr_past is None else jnp.concatenate([kr_past.astype(k_r.dtype), k_r], axis=1)
    o_lat = sweep_query_blocks(lambda qs, qp: mla_attend(qs[0], qs[1], ckv_all, kr_all, qp, kpos), (q_lat, q_rope), pos)
    o_mla = jnp.einsum('bthc,chv->bthv', o_lat, mla_w_uv).reshape(B, T, MLA_HEADS * V_DIM)

    gates = jax.nn.sigmoid(c_gate).reshape(B, T, N_BRANCH, D_MODEL)
    merged = (gates[:, :, 0] * (o_rw @ p_rwkv) + gates[:, :, 1] * (o_sb @ p_sb)
              + gates[:, :, 2] * (o_mla @ p_mla))
    y = x + jnp.einsum('btd,de->bte', merged, w_out)
    return y, S_new, shift_new, k_sb, v_sb, c_kv, k_r


def conv_ffn_sublayer(x, conv_prev, norm_g, w_a, w_b, conv_w, conv_b, w_down):
    T = x.shape[1]
    h = rms_norm(x, norm_g)
    u = h @ w_a
    gate_in = h @ w_b
    u_ext = jnp.concatenate([conv_prev.astype(u.dtype), u], axis=1)
    conv = sum((conv_w[i] * u_ext[:, i:i + T] for i in range(CONV_W)), conv_b)
    y = (jax.nn.gelu(conv) * gate_in) @ w_down
    return x + y, u_ext[:, T:]


def gather_pages(pool, page_table):
    g = pool[page_table]
    return g.reshape((g.shape[0], g.shape[1] * g.shape[2]) + g.shape[3:])


def setup_inputs(seed: int = 0) -> dict:
    key = jax.random.key(seed)
    ks = iter(jax.random.split(key, 64))
    f32 = jnp.float32

    def nrm(shape, scale):
        return jax.random.normal(next(ks), shape, f32) * scale

    def gain(shape):
        return 1.0 + nrm(shape, 0.02)

    n_pages = PAST_LEN // PAGE_SIZE
    n_pool = (DEC_BATCH * n_pages * 5) // 4
    page_table = jax.random.permutation(next(ks), n_pool)[: DEC_BATCH * n_pages].reshape(DEC_BATCH, n_pages).astype(jnp.int32)
    return {
        'x_prompt': nrm((BATCH, SEQ, D_MODEL), 1.0),
        'x_sample': nrm((DEC_BATCH, DEC_SEQ, D_MODEL), 1.0),
        'cache_sb_k': nrm((DEPTH, n_pool, PAGE_SIZE, SB_KV_HEADS, SB_HD), 1.0),
        'cache_sb_v': nrm((DEPTH, n_pool, PAGE_SIZE, SB_KV_HEADS, SB_HD), 1.0),
        'cache_mla_ckv': nrm((DEPTH, n_pool, PAGE_SIZE, KV_LORA), 1.0),
        'cache_mla_krope': nrm((DEPTH, n_pool, PAGE_SIZE, ROPE_DIM), 1.0),
        'state_rwkv': nrm((DEPTH, DEC_BATCH, RW_HEADS, RW_HD, RW_HD), 0.3),
        'state_rwkv_shift': nrm((DEPTH, DEC_BATCH, RW_COLS), 1.0),
        'state_ffn_conv': nrm((DEPTH, DEC_BATCH, CONV_W - 1, D_FF), 1.0),
        'page_table': page_table,
        'norm_mix_g': gain((DEPTH, D_MODEL)),
        'w_in': nrm((DEPTH, D_MODEL, IN_COLS), D_MODEL ** -0.5),
        'rw_mu': jax.random.uniform(next(ks), (DEPTH, RW_COLS), f32),
        'rw_w0': jax.random.uniform(next(ks), (DEPTH, RW_W), f32, -6.0, -1.0),
        'rw_w_up': nrm((DEPTH, W_LORA, RW_W), 0.1),
        'rw_a0': nrm((DEPTH, RW_W), 0.1),
        'rw_a_up': nrm((DEPTH, A_LORA, RW_W), A_LORA ** -0.5),
        'rw_g_up': nrm((DEPTH, G_LORA, RW_W), G_LORA ** -0.5),
        'rw_k_k': 0.85 + nrm((DEPTH, RW_W), 0.05),
        'rw_k_a': 1.0 + nrm((DEPTH, RW_W), 0.05),
        'rw_r_k': nrm((DEPTH, RW_HEADS, RW_HD), 0.1),
        'rw_ln_w': gain((DEPTH, RW_W)),
        'rw_ln_b': nrm((DEPTH, RW_W), 0.02),
        'mla_q_norm_g': gain((DEPTH, Q_LORA)),
        'mla_kv_norm_g': gain((DEPTH, KV_LORA)),
        'mla_w_uq': nrm((DEPTH, Q_LORA, MLA_HEADS, NOPE_DIM + ROPE_DIM), Q_LORA ** -0.5),
        'mla_w_uk': nrm((DEPTH, KV_LORA, MLA_HEADS, NOPE_DIM), KV_LORA ** -0.5),
        'mla_w_uv': nrm((DEPTH, KV_LORA, MLA_HEADS, V_DIM), KV_LORA ** -0.5),
        'p_rwkv': nrm((DEPTH, RW_W, D_MODEL), RW_W ** -0.5),
        'p_sb': nrm((DEPTH, SB_HEADS * SB_HD, D_MODEL), (SB_HEADS * SB_HD) ** -0.5),
        'p_mla': nrm((DEPTH, MLA_HEADS * V_DIM, D_MODEL), (MLA_HEADS * V_DIM) ** -0.5),
        'w_out': nrm((DEPTH, D_MODEL, D_MODEL), D_MODEL ** -0.5),
        'norm_ffn_g': gain((DEPTH, D_MODEL)),
        'ffn_w_a': nrm((DEPTH, D_MODEL, D_FF), D_MODEL ** -0.5),
        'ffn_w_b': nrm((DEPTH, D_MODEL, D_FF), D_MODEL ** -0.5),
        'ffn_conv_w': nrm((DEPTH, CONV_W, D_FF), CONV_W ** -0.5),
        'ffn_conv_b': nrm((DEPTH, D_FF), 0.02),
        'ffn_w_down': nrm((DEPTH, D_FF, D_MODEL), D_FF ** -0.5),
        'norm_final_g': gain((D_MODEL,)),
    }


def reference(x_prompt, x_sample, cache_sb_k, cache_sb_v, cache_mla_ckv, cache_mla_krope, state_rwkv,
              state_rwkv_shift, state_ffn_conv, page_table, norm_mix_g, w_in, rw_mu, rw_w0, rw_w_up, rw_a0,
              rw_a_up, rw_g_up, rw_k_k, rw_k_a, rw_r_k, rw_ln_w, rw_ln_b, mla_q_norm_g, mla_kv_norm_g,
              mla_w_uq, mla_w_uk, mla_w_uv, p_rwkv, p_sb, p_mla, w_out, norm_ffn_g, ffn_w_a, ffn_w_b,
              ffn_conv_w, ffn_conv_b, ffn_w_down, norm_final_g):
    Bp, Tp, _ = x_prompt.shape
    Ts = x_sample.shape[1]
    past_len = page_table.shape[1] * cache_sb_k.shape[2]
    pos_p = jnp.arange(Tp, dtype=jnp.int32)
    pos_s = past_len + jnp.arange(Ts, dtype=jnp.int32)
    dt = x_prompt.dtype
    rw_S0_p = jnp.zeros((Bp, RW_HEADS, RW_HD, RW_HD), dt)
    rw_shift0_p = jnp.zeros((Bp, RW_COLS), dt)
    conv0_p = jnp.zeros((Bp, CONV_W - 1, D_FF), dt)

    xp, xs = x_prompt, x_sample
    p_k, p_v, p_c, p_r, p_S, p_sh, p_cv = [], [], [], [], [], [], []
    s_k, s_v, s_c, s_r, s_S, s_sh, s_cv = [], [], [], [], [], [], []
    for l in range(DEPTH):
        w_l = (norm_mix_g[l], w_in[l], rw_mu[l], rw_w0[l], rw_w_up[l], rw_a0[l], rw_a_up[l], rw_g_up[l],
               rw_k_k[l], rw_k_a[l], rw_r_k[l], rw_ln_w[l], rw_ln_b[l], mla_q_norm_g[l], mla_kv_norm_g[l],
               mla_w_uq[l], mla_w_uk[l], mla_w_uv[l], p_rwkv[l], p_sb[l], p_mla[l], w_out[l])
        f_l = (norm_ffn_g[l], ffn_w_a[l], ffn_w_b[l], ffn_conv_w[l], ffn_conv_b[l], ffn_w_down[l])

        xp, S, sh, k, v, c, r = mixer_sublayer(xp, pos_p, rw_S0_p, rw_shift0_p, None, None, None, None, *w_l)
        p_S.append(S); p_sh.append(sh); p_k.append(k); p_v.append(v); p_c.append(c); p_r.append(r)
        xp, cv = conv_ffn_sublayer(xp, conv0_p, *f_l)
        p_cv.append(cv)

        xs, S, sh, k, v, c, r = mixer_sublayer(
            xs, pos_s, state_rwkv[l], state_rwkv_shift[l],
            gather_pages(cache_sb_k[l], page_table), gather_pages(cache_sb_v[l], page_table),
            gather_pages(cache_mla_ckv[l], page_table), gather_pages(cache_mla_krope[l], page_table), *w_l)
        s_S.append(S); s_sh.append(sh); s_k.append(k); s_v.append(v); s_c.append(c); s_r.append(r)
        xs, cv = conv_ffn_sublayer(xs, state_ffn_conv[l], *f_l)
        s_cv.append(cv)

    y_prompt = rms_norm(xp, norm_final_g)
    y_sample = rms_norm(xs, norm_final_g)
    st = lambda lst: jnp.stack(lst, axis=0)
    return (y_prompt, y_sample,
            st(p_k), st(p_v), st(p_c), st(p_r), st(p_S), st(p_sh), st(p_cv),
            st(s_k), st(s_v), st(s_c), st(s_r), st(s_S), st(s_sh), st(s_cv))
```

```python
import functools
import math

import numpy as np
import jax
import jax.numpy as jnp
from jax import lax
from jax.experimental import pallas as pl
from jax.experimental.pallas import tpu as pltpu

F32 = jnp.float32
BF16 = jnp.bfloat16
HI = lax.Precision.HIGHEST

RW_HEADS, RW_HD = 8, 64
RW_W = RW_HEADS * RW_HD
W_LORA, A_LORA, G_LORA = 64, 64, 160
RW_COLS = 3 * RW_W + W_LORA + A_LORA + G_LORA
RW_GN_EPS = 64e-5
SB_HEADS, SB_KV, SB_HD = 8, 2, 64
SB_GROUP = SB_HEADS // SB_KV
SB_Q = SB_HEADS * SB_HD
SB_KVW = SB_KV * SB_HD
MLA_HEADS = 8
Q_LORA, KV_LORA, NOPE, ROPE, V_DIM = 256, 256, 64, 32, 64
MLA_SCALE = (NOPE + ROPE) ** -0.5
ROPE_BASE = 10000.0
QC = KV_LORA + ROPE
NORM_EPS = 1e-6
CONV_W = 3

VMEM_LIMIT = 56 * 1024 * 1024
LANES = 128


def _cparams(sem):
    return pltpu.CompilerParams(dimension_semantics=sem, vmem_limit_bytes=VMEM_LIMIT)


def _dot(a, b):
    return jnp.dot(a, b, preferred_element_type=F32)


def _dot_hi(a, b):
    return jnp.dot(a, b, preferred_element_type=F32, precision=HI)


def _dot_nt(a, b):
    return lax.dot_general(a, b, (((1,), (1,)), ((), ())), preferred_element_type=F32)


def _split_hi_lo(x):
    hi = x.astype(BF16)
    lo = (x - hi.astype(F32)).astype(BF16)
    return hi, lo


def _sigmoid(x):
    return 1.0 / (1.0 + jnp.exp(-x))


def _softplus(x):
    return jnp.maximum(x, 0.0) + jnp.log1p(jnp.exp(-jnp.abs(x)))


def _full(shape):
    n = len(shape)
    return pl.BlockSpec(shape, lambda *_: (0,) * n)


def _row_tile(m, cap=256):
    t = min(cap, m)
    assert m % t == 0
    return t


def _inproj_kernel(x_ref, g_ref, rope_ref, gq_ref, gkv_ref,
                   w_rw, w_sbq, w_sbk, w_sbv, w_cq, w_ckv, w_kr, w_gate,
                   o_rw, o_sbq, o_sbk, o_sbv, o_cq, o_ckv, o_kr, o_gate):
    x = x_ref[...]
    ms = jnp.mean(x * x, axis=-1, keepdims=True)
    h = (x * lax.rsqrt(ms + NORM_EPS) * g_ref[...]).astype(BF16)
    o_rw[...] = _dot(h, w_rw[...])
    o_sbq[...] = _dot(h, w_sbq[...])
    o_sbk[...] = _dot(h, w_sbk[...])
    o_sbv[...] = _dot(h, w_sbv[...])
    o_gate[...] = _dot(h, w_gate[...])

    def lat_norm(c, g):
        return c * lax.rsqrt(jnp.mean(c * c, axis=-1, keepdims=True) + NORM_EPS) * g

    o_cq[...] = lat_norm(_dot(h, w_cq[...]), gq_ref[...])
    o_ckv[...] = lat_norm(_dot(h, w_ckv[...]), gkv_ref[...])
    kr2 = _dot(h, w_kr[...])
    rope = rope_ref[...]
    o_kr[...] = kr2[:, :ROPE] * rope[:, :ROPE] + kr2[:, ROPE:] * rope[:, ROPE:]


def _inproj(x2, g, rope, gq, gkv, ws):
    m, d = x2.shape
    tm = _row_tile(m)
    widths = (RW_COLS, SB_Q, SB_KVW, SB_KVW, Q_LORA, KV_LORA, ROPE, 3 * d)
    row = lambda n: pl.BlockSpec((tm, n), lambda i: (i, 0))
    in_specs = [row(d), _full((1, d)), row(2 * ROPE), _full((1, Q_LORA)), _full((1, KV_LORA))]
    in_specs += [_full(w.shape) for w in ws]
    return pl.pallas_call(
        _inproj_kernel,
        grid=(m // tm,),
        in_specs=in_specs,
        out_specs=[row(n) for n in widths],
        out_shape=[jax.ShapeDtypeStruct((m, n), F32) for n in widths],
        compiler_params=_cparams(("parallel",)),
    )(x2, g, rope, gq, gkv, *ws)


def _rwkv_pre_kernel(c_ref, p_ref, mu_ref, w0_ref, wup_ref, a0_ref, aup_ref, gup_ref,
                     kk_ref, ka_ref, rk_ref, seg_ref,
                     o_r, o_w, o_k, o_v, o_kkn, o_b, o_g, o_bonus):
    c = c_ref[...]
    xs = c + (p_ref[...] - c) * mu_ref[...]
    r = xs[:, 0:RW_W]
    k = xs[:, RW_W:2 * RW_W]
    v = xs[:, 2 * RW_W:3 * RW_W]
    o0 = 3 * RW_W
    wd = xs[:, o0:o0 + W_LORA]
    ad = xs[:, o0 + W_LORA:o0 + W_LORA + A_LORA]
    gd = xs[:, o0 + W_LORA + A_LORA:RW_COLS]
    wl = w0_ref[...] + _dot_hi(jnp.tanh(wd), wup_ref[...])
    w = -_softplus(-wl) - 0.5
    decay = jnp.exp(-jnp.exp(w))
    a = _sigmoid(a0_ref[...] + _dot_hi(ad, aup_ref[...]))
    g = _dot_hi(_sigmoid(gd), gup_ref[...])
    seg = seg_ref[...]
    kk = k * kk_ref[...]
    kk = kk * lax.rsqrt(jnp.maximum(_dot_hi(kk * kk, seg), 1e-24))
    k2 = k * (1.0 + (a - 1.0) * ka_ref[...])
    o_r[...] = r
    o_w[...] = decay
    o_k[...] = k2
    o_v[...] = v
    o_kkn[...] = -kk
    o_b[...] = kk * a
    o_g[...] = g
    o_bonus[...] = _dot_hi(r * k2 * rk_ref[...], seg) * v


def _rwkv_pre(c2, p2, mu, w0, wup, a0, aup, gup, kk, ka, rk, seg):
    m = c2.shape[0]
    tm = _row_tile(m)
    rowc = pl.BlockSpec((tm, RW_COLS), lambda i: (i, 0))
    roww = pl.BlockSpec((tm, RW_W), lambda i: (i, 0))
    vec = _full((1, RW_W))
    return pl.pallas_call(
        _rwkv_pre_kernel,
        grid=(m // tm,),
        in_specs=[rowc, rowc, _full((1, RW_COLS)), vec, _full(wup.shape), vec, _full(aup.shape),
                  _full(gup.shape), vec, vec, vec, _full(seg.shape)],
        out_specs=[roww] * 8,
        out_shape=[jax.ShapeDtypeStruct((m, RW_W), F32)] * 8,
        compiler_params=_cparams(("parallel",)),
    )(c2, p2, mu, w0, wup, a0, aup, gup, kk, ka, rk, seg)


RW_PAIRS = RW_HEADS // 2
RW_SUB = 64


def _rwkv_scan_kernel(bb, tc, r_ref, w_ref, k_ref, v_ref, kkn_ref, b_ref, s0_ref, eye_ref, ones2_ref,
                      ones1_ref, o_ref, sfin_ref, s_scr, o_scr):
    ci = pl.program_id(1)
    groups = [(bi, hp) for bi in range(bb) for hp in range(RW_PAIRS)]

    @pl.when(ci == 0)
    def _():
        o_scr[...] = jnp.zeros_like(o_scr)
        for gi, (bi, hp) in enumerate(groups):
            s_scr[gi] = jnp.concatenate([s0_ref[bi, 2 * hp], s0_ref[bi, 2 * hp + 1]], axis=1)

    eye2 = eye_ref[...]
    ones2 = ones2_ref[...]
    ones1 = ones1_ref[...]
    lane = lax.broadcasted_iota(jnp.int32, (RW_HD, LANES), 1) % RW_HD
    sub = min(RW_SUB, tc)

    def step8(t0, t8, _):
        base = pl.multiple_of(t0 + t8 * 8, 8)
        for gi, (bi, hp) in enumerate(groups):
            ls = pl.ds(hp * LANES, LANES)
            tile = lambda ref: ref[bi, pl.ds(base, 8), ls]
            kkn8, v8, w8, b8, k8, r8 = (tile(ref) for ref in (kkn_ref, v_ref, w_ref, b_ref, k_ref, r_ref))
            s = s_scr[gi]
            oacc = o_scr[gi]
            for i in range(8):
                row = lambda x: x[i:i + 1, :]
                x = jnp.concatenate([s * row(kkn8), eye2 * row(v8)], axis=1)
                hi, lo = _split_hi_lo(x)
                red = _dot(hi, ones2) + _dot(lo, ones2)
                s = s * row(w8) + red[:, :LANES] * row(b8) + red[:, LANES:] * row(k8)
                hi, lo = _split_hi_lo(s * row(r8))
                ob = _dot(hi, ones1) + _dot(lo, ones1)
                oacc = jnp.where(lane == t8 * 8 + i, ob, oacc)
            s_scr[gi] = s
            o_scr[gi] = oacc
        return 0

    def chunk(c, _):
        t0 = pl.multiple_of(c * sub, sub)
        lax.fori_loop(0, sub // 8, functools.partial(step8, t0), 0)
        for gi, (bi, hp) in enumerate(groups):
            ot = o_scr[gi].T
            o_ref[bi, pl.ds(t0, sub), hp * LANES:hp * LANES + RW_HD] = ot[0:sub]
            o_ref[bi, pl.ds(t0, sub), hp * LANES + RW_HD:(hp + 1) * LANES] = ot[RW_HD:RW_HD + sub]
        return 0

    lax.fori_loop(0, tc // sub, chunk, 0)

    @pl.when(ci == pl.num_programs(1) - 1)
    def _():
        for gi, (bi, hp) in enumerate(groups):
            s = s_scr[gi]
            sfin_ref[bi, 2 * hp] = s[:, :RW_HD]
            sfin_ref[bi, 2 * hp + 1] = s[:, RW_HD:]


def _rwkv_scan(r, w, k, v, kkn, b, s0, consts):
    bsz, t, _ = r.shape
    bb = 4
    tc = min(t, 256)
    assert bsz % bb == 0 and t % tc == 0 and (tc % RW_SUB == 0 or tc < RW_SUB)
    seq = pl.BlockSpec((bb, tc, RW_W), lambda i, c: (i, c, 0))
    st = pl.BlockSpec((bb, RW_HEADS, RW_HD, RW_HD), lambda i, c: (i, 0, 0, 0))
    eye2, ones2, ones1 = consts
    return pl.pallas_call(
        functools.partial(_rwkv_scan_kernel, bb, tc),
        grid=(bsz // bb, t // tc),
        in_specs=[seq] * 6 + [st, _full(eye2.shape), _full(ones2.shape), _full(ones1.shape)],
        out_specs=[seq, st],
        out_shape=[jax.ShapeDtypeStruct((bsz, t, RW_W), F32),
                   jax.ShapeDtypeStruct((bsz, RW_HEADS, RW_HD, RW_HD), F32)],
        scratch_shapes=[pltpu.VMEM((bb * RW_PAIRS, RW_HD, LANES), F32),
                        pltpu.VMEM((bb * RW_PAIRS, RW_HD, LANES), F32)],
        compiler_params=_cparams(("parallel", "arbitrary")),
    )(r, w, k, v, kkn, b, s0, eye2, ones2, ones1)


def _rwkv_post_kernel(o_ref, bonus_ref, g_ref, lnw_ref, lnb_ref, seg_ref, out_ref):
    o = o_ref[...]
    seg = seg_ref[...]
    mean = _dot_hi(o, seg) * (1.0 / RW_HD)
    d = o - mean
    var = _dot_hi(d * d, seg) * (1.0 / RW_HD)
    y = d * lax.rsqrt(var + RW_GN_EPS) * lnw_ref[...] + lnb_ref[...]
    out_ref[...] = (y + bonus_ref[...]) * g_ref[...]


def _rwkv_post(o2, bonus, g, lnw, lnb, seg):
    m = o2.shape[0]
    tm = _row_tile(m)
    roww = pl.BlockSpec((tm, RW_W), lambda i: (i, 0))
    vec = _full((1, RW_W))
    return pl.pallas_call(
        _rwkv_post_kernel,
        grid=(m // tm,),
        in_specs=[roww, roww, roww, vec, vec, _full(seg.shape)],
        out_specs=roww,
        out_shape=jax.ShapeDtypeStruct((m, RW_W), F32),
        compiler_params=_cparams(("parallel",)),
    )(o2, bonus, g, lnw, lnb, seg)


SB_KB = 128


def _sb_block(q, kb, vb, tri, run, acc, mask):
    z = _dot_nt(q, kb.astype(BF16))
    l1m = -_softplus(z)
    if mask is not None:
        l1m = jnp.where(mask, l1m, 0.0)
    hi, lo = _split_hi_lo(l1m)
    after = _dot(hi, tri) + _dot(lo, tri)
    a = jnp.exp(z + l1m + after + run)
    if mask is not None:
        a = jnp.where(mask, a, 0.0)
    acc = acc + _dot(a.astype(BF16), vb.astype(BF16))
    run = run + after[:, 0:1] + l1m[:, 0:1]
    return run, acc


def _sb_prompt_kernel(tq, q_ref, k_ref, v_ref, tri_ref, o_ref, q_scr, run_scr, acc_scr):
    qi = pl.program_id(1)
    j = pl.program_id(2)
    rows = SB_GROUP * tq
    nsub = tq // SB_KB

    @pl.when(j == 0)
    def _():
        for kvh in range(SB_KV):
            for g in range(SB_GROUP):
                h = kvh * SB_GROUP + g
                q_scr[kvh, g * tq:(g + 1) * tq, :] = (q_ref[:, h * SB_HD:(h + 1) * SB_HD] * SB_HD ** -0.5).astype(BF16)
        run_scr[...] = jnp.zeros_like(run_scr)
        acc_scr[...] = jnp.zeros_like(acc_scr)

    def sweep(diag):
        tri = tri_ref[...]
        for kvh in range(SB_KV):
            q = q_scr[kvh]
            run = run_scr[kvh]
            acc = acc_scr[kvh]
            for sb in reversed(range(nsub)):
                kb = k_ref[sb * SB_KB:(sb + 1) * SB_KB, kvh * SB_HD:(kvh + 1) * SB_HD]
                vb = v_ref[sb * SB_KB:(sb + 1) * SB_KB, kvh * SB_HD:(kvh + 1) * SB_HD]
                mask = None
                if diag:
                    qpos = lax.broadcasted_iota(jnp.int32, (rows, SB_KB), 0) % tq
                    kpos = lax.broadcasted_iota(jnp.int32, (rows, SB_KB), 1) + sb * SB_KB
                    mask = kpos < qpos
                run, acc = _sb_block(q, kb, vb, tri, run, acc, mask)
            run_scr[kvh] = run
            acc_scr[kvh] = acc

    @pl.when(j == 0)
    def _():
        sweep(True)

    @pl.when(jnp.logical_and(j > 0, j <= qi))
    def _():
        sweep(False)

    @pl.when(j == qi)
    def _():
        for kvh in range(SB_KV):
            for g in range(SB_GROUP):
                h = kvh * SB_GROUP + g
                o_ref[:, h * SB_HD:(h + 1) * SB_HD] = acc_scr[kvh, g * tq:(g + 1) * tq, :]


def _sb_prompt(q, k, v, tri):
    bsz, t, _ = q.shape
    tq = min(256, t)
    nq = t // tq
    rows = SB_GROUP * tq
    kv_spec = pl.BlockSpec((None, tq, SB_KVW), lambda b, i, j: (b, jnp.maximum(i - j, 0), 0))
    return pl.pallas_call(
        functools.partial(_sb_prompt_kernel, tq),
        grid=(bsz, nq, nq),
        in_specs=[pl.BlockSpec((None, tq, SB_Q), lambda b, i, j: (b, i, 0)), kv_spec, kv_spec,
                  _full(tri.shape)],
        out_specs=pl.BlockSpec((None, tq, SB_Q), lambda b, i, j: (b, i, 0)),
        out_shape=jax.ShapeDtypeStruct((bsz, t, SB_Q), F32),
        scratch_shapes=[pltpu.VMEM((SB_KV, rows, SB_HD), BF16),
                        pltpu.VMEM((SB_KV, rows, 1), F32),
                        pltpu.VMEM((SB_KV, rows, SB_HD), F32)],
        compiler_params=_cparams(("parallel", "parallel", "arbitrary")),
    )(q, k, v, tri)


SB_PAGES = 8


def _sb_sample_kernel(t_new, n_pg, *refs):
    pt_ref = refs[0]
    q_ref, kn_ref, vn_ref, tri_ref = refs[1:5]
    k_refs = refs[5:5 + n_pg]
    v_refs = refs[5 + n_pg:5 + 2 * n_pg]
    o_ref, run_scr, acc_scr = refs[5 + 2 * n_pg:]
    del pt_ref
    j = pl.program_id(1)
    rows = q_ref.shape[0]
    tri = tri_ref[...]
    q = q_ref[...].astype(BF16)

    def block(kb, vb, run, acc, mask):
        z = _dot_nt(q, kb.astype(BF16))
        l1m = -_softplus(z)
        if mask is not None:
            l1m = jnp.where(mask, l1m, 0.0)
        hi, lo = _split_hi_lo(l1m)
        after = _dot(hi, tri) + _dot(lo, tri)
        a = jnp.exp(z + l1m + after + run)
        if mask is not None:
            a = jnp.where(mask, a, 0.0)
        acc = acc + _dot(a.astype(BF16), vb.astype(BF16))
        run = run + after[:, 0:1] + l1m[:, 0:1]
        return run, acc

    @pl.when(j == 0)
    def _():
        qpos = lax.broadcasted_iota(jnp.int32, (rows, SB_KB), 0) % t_new
        kpos = lax.broadcasted_iota(jnp.int32, (rows, SB_KB), 1)
        run, acc = block(kn_ref[...], vn_ref[...], jnp.zeros((rows, 1), F32),
                         jnp.zeros((rows, LANES), F32), kpos < qpos)
        run_scr[...] = run
        acc_scr[...] = acc

    run = run_scr[...]
    acc = acc_scr[...]
    for p in range(n_pg):
        run, acc = block(k_refs[p][...], v_refs[p][...], run, acc, None)
    run_scr[...] = run
    acc_scr[...] = acc

    @pl.when(j == pl.num_programs(1) - 1)
    def _():
        o_ref[...] = acc


def _sb_sample(layer, page_table, qpad, kn_pad, vn_pad, cache_k, cache_v, tri, t_new):
    bsz, rows, _ = qpad.shape
    n_pages = page_table.shape[1]
    n_pg = min(SB_PAGES, n_pages)
    assert n_pages % n_pg == 0
    steps = n_pages // n_pg
    page = cache_k.shape[2]

    def page_spec(p):
        return pl.BlockSpec((None, None, page, LANES),
                            lambda b, j, pt: (layer, pt[b, n_pages - 1 - (j * n_pg + p)], 0, 0))

    per_b = lambda n: pl.BlockSpec((None, n, LANES), lambda b, j, pt: (b, 0, 0))
    grid_spec = pltpu.PrefetchScalarGridSpec(
        num_scalar_prefetch=1,
        grid=(bsz, steps),
        in_specs=[per_b(rows), per_b(SB_KB), per_b(SB_KB), pl.BlockSpec(tri.shape, lambda b, j, pt: (0, 0))]
        + [page_spec(p) for p in range(n_pg)] * 2,
        out_specs=per_b(rows),
        scratch_shapes=[pltpu.VMEM((rows, 1), F32), pltpu.VMEM((rows, LANES), F32)],
    )
    return pl.pallas_call(
        functools.partial(_sb_sample_kernel, t_new, n_pg),
        grid_spec=grid_spec,
        out_shape=jax.ShapeDtypeStruct((bsz, rows, LANES), F32),
        compiler_params=_cparams(("parallel", "arbitrary")),
    )(page_table, qpad, kn_pad, vn_pad, tri, *([cache_k] * n_pg), *([cache_v] * n_pg))


def _mla_qprep_kernel(cq_ref, rope_ref, wqn_ref, wqr_ref, wukt_ref, o_ref):
    cq = cq_ref[...].astype(BF16)
    rope = rope_ref[...]
    for h in range(MLA_HEADS):
        qn = _dot(cq, wqn_ref[h])
        qr2 = _dot(cq, wqr_ref[h])
        o_ref[h, :, 0:KV_LORA] = _dot(qn.astype(BF16), wukt_ref[h])
        o_ref[h, :, KV_LORA:QC] = qr2[:, :ROPE] * rope[:, :ROPE] + qr2[:, ROPE:] * rope[:, ROPE:]


def _mla_qprep(cq, rope, wqn, wqr, wukt):
    m = cq.shape[0]
    tm = _row_tile(m)
    return pl.pallas_call(
        _mla_qprep_kernel,
        grid=(m // tm,),
        in_specs=[pl.BlockSpec((tm, Q_LORA), lambda i: (i, 0)), pl.BlockSpec((tm, 2 * ROPE), lambda i: (i, 0)),
                  _full(wqn.shape), _full(wqr.shape), _full(wukt.shape)],
        out_specs=pl.BlockSpec((MLA_HEADS, tm, QC), lambda i: (0, i, 0)),
        out_shape=jax.ShapeDtypeStruct((MLA_HEADS, m, QC), F32),
        compiler_params=_cparams(("parallel",)),
    )(cq, rope, wqn, wqr, wukt)


def _flash_block(ql, qr, ckv, kr, m_i, l_i, acc, mask):
    cb = ckv.astype(BF16)
    s = (_dot_nt(ql, cb) + _dot_nt(qr, kr.astype(BF16))) * MLA_SCALE
    if mask is not None:
        s = jnp.where(mask, s, -jnp.inf)
    m_new = jnp.maximum(m_i, jnp.max(s, axis=-1, keepdims=True))
    alpha = jnp.exp(m_i - m_new)
    p = jnp.exp(s - m_new)
    l_i = alpha * l_i + jnp.sum(p, axis=-1, keepdims=True)
    acc = alpha * acc + _dot(p.astype(BF16), cb)
    return m_new, l_i, acc


def _mla_prompt_kernel(tq, q_ref, ckv_ref, kr_ref, o_ref, ql_scr, qr_scr, m_scr, l_scr, acc_scr):
    qi = pl.program_id(1)
    j = pl.program_id(2)
    rows = MLA_HEADS * tq

    @pl.when(j == 0)
    def _():
        for h in range(MLA_HEADS):
            ql_scr[h * tq:(h + 1) * tq, :] = q_ref[h, :, 0:KV_LORA].astype(BF16)
            qr_scr[h * tq:(h + 1) * tq, :] = q_ref[h, :, KV_LORA:QC].astype(BF16)
        m_scr[...] = jnp.full_like(m_scr, -jnp.inf)
        l_scr[...] = jnp.zeros_like(l_scr)
        acc_scr[...] = jnp.zeros_like(acc_scr)

    def sweep(diag):
        mask = None
        if diag:
            qpos = lax.broadcasted_iota(jnp.int32, (rows, tq), 0) % tq
            kpos = lax.broadcasted_iota(jnp.int32, (rows, tq), 1)
            mask = kpos <= qpos
        m_new, l_new, acc = _flash_block(ql_scr[...], qr_scr[...], ckv_ref[...], kr_ref[...],
                                         m_scr[...], l_scr[...], acc_scr[...], mask)
        m_scr[...] = m_new
        l_scr[...] = l_new
        acc_scr[...] = acc

    @pl.when(j == 0)
    def _():
        sweep(True)

    @pl.when(jnp.logical_and(j > 0, j <= qi))
    def _():
        sweep(False)

    @pl.when(j == qi)
    def _():
        o = acc_scr[...] / l_scr[...]
        for h in range(MLA_HEADS):
            o_ref[h] = o[h * tq:(h + 1) * tq, :]


def _mla_prompt(qcat, ckv, kr, bsz, t):
    tq = min(256, t)
    nq = t // tq
    rows = MLA_HEADS * tq
    kblk = lambda b, i, j: (b * nq + jnp.maximum(i - j, 0), 0)
    return pl.pallas_call(
        functools.partial(_mla_prompt_kernel, tq),
        grid=(bsz, nq, nq),
        in_specs=[pl.BlockSpec((MLA_HEADS, tq, QC), lambda b, i, j: (0, b * nq + i, 0)),
                  pl.BlockSpec((tq, KV_LORA), kblk), pl.BlockSpec((tq, ROPE), kblk)],
        out_specs=pl.BlockSpec((MLA_HEADS, tq, KV_LORA), lambda b, i, j: (0, b * nq + i, 0)),
        out_shape=jax.ShapeDtypeStruct((MLA_HEADS, bsz * t, KV_LORA), F32),
        scratch_shapes=[pltpu.VMEM((rows, KV_LORA), BF16), pltpu.VMEM((rows, ROPE), BF16),
                        pltpu.VMEM((rows, 1), F32), pltpu.VMEM((rows, 1), F32),
                        pltpu.VMEM((rows, KV_LORA), F32)],
        compiler_params=_cparams(("parallel", "parallel", "arbitrary")),
    )(qcat, ckv, kr)


def _mla_sample_kernel(t_new, n_pg, *refs):
    pt_ref = refs[0]
    q_ref, cn_ref, rn_ref = refs[1:4]
    c_refs = refs[4:4 + n_pg]
    r_refs = refs[4 + n_pg:4 + 2 * n_pg]
    o_ref, m_scr, l_scr, acc_scr = refs[4 + 2 * n_pg:]
    del pt_ref
    j = pl.program_id(1)
    rows = MLA_HEADS * t_new
    q = q_ref[...].reshape(rows, QC)
    ql = q[:, 0:KV_LORA].astype(BF16)
    qr = q[:, KV_LORA:QC].astype(BF16)

    @pl.when(j == 0)
    def _():
        s_new = cn_ref.shape[0]
        qpos = lax.broadcasted_iota(jnp.int32, (rows, s_new), 0) % t_new
        kpos = lax.broadcasted_iota(jnp.int32, (rows, s_new), 1)
        m0 = jnp.full((rows, 1), -jnp.inf, F32)
        m_new, l_new, acc = _flash_block(ql, qr, cn_ref[...], rn_ref[...], m0, jnp.zeros((rows, 1), F32),
                                         jnp.zeros((rows, KV_LORA), F32), kpos <= qpos)
        m_scr[...] = m_new
        l_scr[...] = l_new
        acc_scr[...] = acc

    m_i, l_i, acc = m_scr[...], l_scr[...], acc_scr[...]
    for p in range(n_pg):
        m_i, l_i, acc = _flash_block(ql, qr, c_refs[p][...], r_refs[p][...], m_i, l_i, acc, None)
    m_scr[...] = m_i
    l_scr[...] = l_i
    acc_scr[...] = acc

    @pl.when(j == pl.num_programs(1) - 1)
    def _():
        o_ref[...] = (acc / l_i).reshape(MLA_HEADS, t_new, KV_LORA)


def _mla_sample(layer, page_table, qcat, cn_pad, rn_pad, cache_c, cache_r, t_new):
    bsz = page_table.shape[0]
    n_pages = page_table.shape[1]
    n_pg = min(SB_PAGES, n_pages)
    steps = n_pages // n_pg
    page = cache_c.shape[2]
    rows = MLA_HEADS * t_new
    pidx = lambda p: (lambda b, j, pt: (layer, pt[b, j * n_pg + p], 0, 0))
    grid_spec = pltpu.PrefetchScalarGridSpec(
        num_scalar_prefetch=1,
        grid=(bsz, steps),
        in_specs=[pl.BlockSpec((MLA_HEADS, t_new, QC), lambda b, j, pt: (0, b, 0)),
                  pl.BlockSpec((None, cn_pad.shape[1], KV_LORA), lambda b, j, pt: (b, 0, 0)),
                  pl.BlockSpec((None, rn_pad.shape[1], ROPE), lambda b, j, pt: (b, 0, 0))]
        + [pl.BlockSpec((None, None, page, KV_LORA), pidx(p)) for p in range(n_pg)]
        + [pl.BlockSpec((None, None, page, ROPE), pidx(p)) for p in range(n_pg)],
        out_specs=pl.BlockSpec((MLA_HEADS, t_new, KV_LORA), lambda b, j, pt: (0, b, 0)),
        scratch_shapes=[pltpu.VMEM((rows, 1), F32), pltpu.VMEM((rows, 1), F32),
                        pltpu.VMEM((rows, KV_LORA), F32)],
    )
    return pl.pallas_call(
        functools.partial(_mla_sample_kernel, t_new, n_pg),
        grid_spec=grid_spec,
        out_shape=jax.ShapeDtypeStruct((MLA_HEADS, bsz * t_new, KV_LORA), F32),
        compiler_params=_cparams(("parallel", "arbitrary")),
    )(page_table, qcat, cn_pad, rn_pad, *([cache_c] * n_pg), *([cache_r] * n_pg))


def _merge_kernel(x_ref, gate_ref, orw_ref, osb_ref, olat_ref, wuv_ref, prw_ref, psb_ref, pmla_ref,
                  wout_ref, y_ref):
    d = x_ref.shape[1]
    gate = gate_ref[...]
    o_mla = jnp.concatenate(
        [_dot(olat_ref[h].astype(BF16), wuv_ref[h]) for h in range(MLA_HEADS)], axis=1)
    merged = (_sigmoid(gate[:, 0:d]) * _dot(orw_ref[...].astype(BF16), prw_ref[...])
              + _sigmoid(gate[:, d:2 * d]) * _dot(osb_ref[...].astype(BF16), psb_ref[...])
              + _sigmoid(gate[:, 2 * d:3 * d]) * _dot(o_mla.astype(BF16), pmla_ref[...]))
    y_ref[...] = x_ref[...] + _dot(merged.astype(BF16), wout_ref[...])


def _merge(x2, gate, o_rw, o_sb, o_lat, wuv, prw, psb, pmla, wout):
    m, d = x2.shape
    tm = _row_tile(m)
    row = lambda n: pl.BlockSpec((tm, n), lambda i: (i, 0))
    return pl.pallas_call(
        _merge_kernel,
        grid=(m // tm,),
        in_specs=[row(d), row(3 * d), row(RW_W), row(SB_Q),
                  pl.BlockSpec((MLA_HEADS, tm, KV_LORA), lambda i: (0, i, 0)),
                  _full(wuv.shape), _full(prw.shape), _full(psb.shape), _full(pmla.shape), _full(wout.shape)],
        out_specs=row(d),
        out_shape=jax.ShapeDtypeStruct((m, d), F32),
        compiler_params=_cparams(("parallel",)),
    )(x2, gate, o_rw, o_sb, o_lat, wuv, prw, psb, pmla, wout)


def _ffn_up_kernel(x_ref, g_ref, wa_ref, wb_ref, u_ref, gin_ref):
    x = x_ref[...]
    ms = jnp.mean(x * x, axis=-1, keepdims=True)
    h = (x * lax.rsqrt(ms + NORM_EPS) * g_ref[...]).astype(BF16)
    u_ref[...] = _dot(h, wa_ref[...])
    gin_ref[...] = _dot(h, wb_ref[...])


def _ffn_up(x2, g, wa, wb):
    m, d = x2.shape
    f = wa.shape[1]
    tm = _row_tile(m)
    row = lambda n: pl.BlockSpec((tm, n), lambda i: (i, 0))
    return pl.pallas_call(
        _ffn_up_kernel,
        grid=(m // tm,),
        in_specs=[row(d), _full((1, d)), _full(wa.shape), _full(wb.shape)],
        out_specs=[row(f), row(f)],
        out_shape=[jax.ShapeDtypeStruct((m, f), F32)] * 2,
        compiler_params=_cparams(("parallel",)),
    )(x2, g, wa, wb)


def _gelu_tanh(x):
    c = math.sqrt(2.0 / math.pi)
    return x * (0.5 * (1.0 + jnp.tanh(c * (x + 0.044715 * (x * x * x)))))


def _ffn_down_kernel(x_ref, u_ref, u1_ref, u2_ref, gin_ref, cw_ref, cb_ref, wd_ref, y_ref):
    cw = cw_ref[...]
    conv = cb_ref[...] + cw[0:1] * u2_ref[...] + cw[1:2] * u1_ref[...] + cw[2:3] * u_ref[...]
    act = (_gelu_tanh(conv) * gin_ref[...]).astype(BF16)
    y_ref[...] = x_ref[...] + _dot(act, wd_ref[...])


def _ffn_down(x2, u, u1, u2, gin, cw, cb, wd):
    m, d = x2.shape
    f = u.shape[1]
    tm = _row_tile(m)
    row = lambda n: pl.BlockSpec((tm, n), lambda i: (i, 0))
    return pl.pallas_call(
        _ffn_down_kernel,
        grid=(m // tm,),
        in_specs=[row(d), row(f), row(f), row(f), row(f), _full(cw.shape), _full((1, f)), _full(wd.shape)],
        out_specs=row(d),
        out_shape=jax.ShapeDtypeStruct((m, d), F32),
        compiler_params=_cparams(("parallel",)),
    )(x2, u, u1, u2, gin, cw, cb, wd)


def _rmsnorm_kernel(x_ref, g_ref, y_ref):
    x = x_ref[...]
    y_ref[...] = x * lax.rsqrt(jnp.mean(x * x, axis=-1, keepdims=True) + NORM_EPS) * g_ref[...]


def _rmsnorm(x2, g):
    m, d = x2.shape
    tm = _row_tile(m, 512)
    return pl.pallas_call(
        _rmsnorm_kernel,
        grid=(m // tm,),
        in_specs=[pl.BlockSpec((tm, d), lambda i: (i, 0)), _full((1, d))],
        out_specs=pl.BlockSpec((tm, d), lambda i: (i, 0)),
        out_shape=jax.ShapeDtypeStruct((m, d), F32),
        compiler_params=_cparams(("parallel",)),
    )(x2, g)


def _block_ones(n, blk, dtype):
    i = np.arange(n)
    return jnp.asarray((i[:, None] // blk) == (i[None, :] // blk), dtype)


def _constants():
    i = np.arange(SB_KB)
    return dict(
        seg=_block_ones(RW_W, RW_HD, F32),
        eye2=jnp.asarray(np.concatenate([np.eye(RW_HD), np.eye(RW_HD)], axis=1), F32),
        ones2=_block_ones(2 * LANES, RW_HD, BF16),
        ones1=_block_ones(LANES, RW_HD, BF16),
        tri=jnp.asarray(i[:, None] > i[None, :], BF16),
    )


def _rope_table(pos):
    half = ROPE // 2
    inv = jnp.exp(-math.log(ROPE_BASE) * jnp.arange(half, dtype=F32) / half)
    ang = pos.astype(F32)[:, None] * inv[None, :]
    cos, sin = jnp.cos(ang), jnp.sin(ang)
    return jnp.concatenate([cos, cos, -sin, sin], axis=1)


def _rot_cols(w):
    half = ROPE // 2
    return jnp.concatenate([w[..., half:], w[..., :half]], axis=-1)


def _layer_weights(l, d, p):
    w_in = p['w_in'][l]
    offs = np.cumsum([0, RW_COLS, SB_Q, SB_KVW, SB_KVW, Q_LORA, KV_LORA, ROPE, 3 * d])
    parts = [w_in[:, offs[i]:offs[i + 1]] for i in range(8)]
    parts[6] = jnp.concatenate([parts[6], _rot_cols(parts[6])], axis=1)
    w_uq = p['mla_w_uq'][l]
    wqr = w_uq[:, :, NOPE:]
    row = lambda v: v.reshape(1, -1)
    return dict(
        norm_g=row(p['norm_mix_g'][l]),
        inproj=[w.astype(BF16) for w in parts],
        gq=row(p['mla_q_norm_g'][l]), gkv=row(p['mla_kv_norm_g'][l]),
        mu=row(p['rw_mu'][l]), w0=row(p['rw_w0'][l]), wup=p['rw_w_up'][l], a0=row(p['rw_a0'][l]),
        aup=p['rw_a_up'][l], gup=p['rw_g_up'][l], kk=row(p['rw_k_k'][l]), ka=row(p['rw_k_a'][l]),
        rk=row(p['rw_r_k'][l]), lnw=row(p['rw_ln_w'][l]), lnb=row(p['rw_ln_b'][l]),
        wqn=jnp.transpose(w_uq[:, :, :NOPE], (1, 0, 2)).astype(BF16),
        wqr=jnp.transpose(jnp.concatenate([wqr, _rot_cols(wqr)], axis=-1), (1, 0, 2)).astype(BF16),
        wukt=jnp.transpose(p['mla_w_uk'][l], (1, 2, 0)).astype(BF16),
        wuv=jnp.transpose(p['mla_w_uv'][l], (1, 0, 2)).astype(BF16),
        prw=p['p_rwkv'][l].astype(BF16), psb=p['p_sb'][l].astype(BF16), pmla=p['p_mla'][l].astype(BF16),
        wout=p['w_out'][l].astype(BF16),
        ffn_g=row(p['norm_ffn_g'][l]), wa=p['ffn_w_a'][l].astype(BF16), wb=p['ffn_w_b'][l].astype(BF16),
        cw=p['ffn_conv_w'][l], cb=row(p['ffn_conv_b'][l]), wd=p['ffn_w_down'][l].astype(BF16),
    )


def _shift_rows(x3, first, n):
    assert x3.shape[1] > n
    return jnp.concatenate([first, x3[:, :x3.shape[1] - n]], axis=1)


def _pad_rows(x3, n):
    return jnp.pad(x3, ((0, 0), (0, n - x3.shape[1]), (0, 0)))


def _layer(x2, bsz, t, rope, w, cst, rw_s0, rw_shift0, conv_prev, past):
    m, d = x2.shape
    c_rw, sbq, sbk, sbv, cq, ckv, kr, gate = _inproj(x2, w['norm_g'], rope, w['gq'], w['gkv'], w['inproj'])

    c3 = c_rw.reshape(bsz, t, RW_COLS)
    prev = _shift_rows(c3, rw_shift0[:, None, :], 1).reshape(m, RW_COLS)
    r, dec, k2, v, kkn, b, g, bonus = _rwkv_pre(c_rw, prev, w['mu'], w['w0'], w['wup'], w['a0'], w['aup'],
                                                  w['gup'], w['kk'], w['ka'], w['rk'], cst['seg'])
    sq = lambda a: a.reshape(bsz, t, RW_W)
    o, s_new = _rwkv_scan(sq(r), sq(dec), sq(k2), sq(v), sq(kkn), sq(b), rw_s0,
                          (cst['eye2'], cst['ones2'], cst['ones1']))
    o_rw = _rwkv_post(o.reshape(m, RW_W), bonus, g, w['lnw'], w['lnb'], cst['seg'])

    qcat = _mla_qprep(cq, rope, w['wqn'], w['wqr'], w['wukt'])
    if past is None:
        o_sb = _sb_prompt(sbq.reshape(bsz, t, SB_Q), sbk.reshape(bsz, t, SB_KVW), sbv.reshape(bsz, t, SB_KVW),
                          cst['tri']).reshape(m, SB_Q)
        o_lat = _mla_prompt(qcat, ckv, kr, bsz, t)
    else:
        layer, page_table, cache_k, cache_v, cache_c, cache_r = past
        q5 = sbq.reshape(bsz, t, SB_KV, SB_GROUP, SB_HD) * SB_HD ** -0.5
        q5 = jnp.transpose(q5, (0, 2, 3, 1, 4)).reshape(bsz, SB_KV, SB_GROUP * t, SB_HD)
        z = jnp.zeros_like(q5[:, 0])
        qpad = jnp.concatenate([jnp.concatenate([q5[:, 0], z], axis=-1),
                                jnp.concatenate([z, q5[:, 1]], axis=-1)], axis=1)
        o_pad = _sb_sample(layer, page_table, qpad, _pad_rows(sbk.reshape(bsz, t, SB_KVW), SB_KB),
                           _pad_rows(sbv.reshape(bsz, t, SB_KVW), SB_KB), cache_k, cache_v, cst['tri'], t)
        rows = SB_GROUP * t
        o4 = jnp.stack([o_pad[:, :rows, :SB_HD], o_pad[:, rows:, SB_HD:]], axis=1)
        o_sb = jnp.transpose(o4.reshape(bsz, SB_KV, SB_GROUP, t, SB_HD), (0, 3, 1, 2, 4)).reshape(m, SB_Q)
        o_lat = _mla_sample(layer, page_table, qcat, _pad_rows(ckv.reshape(bsz, t, KV_LORA), SB_KB),
                            _pad_rows(kr.reshape(bsz, t, ROPE), SB_KB), cache_c, cache_r, t)

    y = _merge(x2, gate, o_rw, o_sb, o_lat, w['wuv'], w['prw'], w['psb'], w['pmla'], w['wout'])

    u, gin = _ffn_up(y, w['ffn_g'], w['wa'], w['wb'])
    f = u.shape[1]
    u3 = u.reshape(bsz, t, f)
    u1 = _shift_rows(u3, conv_prev[:, 1:], 1).reshape(m, f)
    u2 = _shift_rows(u3, conv_prev, 2).reshape(m, f)
    y = _ffn_down(y, u, u1, u2, gin, w['cw'], w['cb'], w['wd'])
    conv_new = jnp.concatenate([conv_prev, u3], axis=1)[:, t:]
    states = (s_new, c3[:, -1], sbk.reshape(bsz, t, SB_KV, SB_HD), sbv.reshape(bsz, t, SB_KV, SB_HD),
              ckv.reshape(bsz, t, KV_LORA), kr.reshape(bsz, t, ROPE), conv_new)
    return y, states


def kernel(x_prompt, x_sample, cache_sb_k, cache_sb_v, cache_mla_ckv, cache_mla_krope, state_rwkv, state_rwkv_shift, state_ffn_conv, page_table, norm_mix_g, w_in, rw_mu, rw_w0, rw_w_up, rw_a0, rw_a_up, rw_g_up, rw_k_k, rw_k_a, rw_r_k, rw_ln_w, rw_ln_b, mla_q_norm_g, mla_kv_norm_g, mla_w_uq, mla_w_uk, mla_w_uv, p_rwkv, p_sb, p_mla, w_out, norm_ffn_g, ffn_w_a, ffn_w_b, ffn_conv_w, ffn_conv_b, ffn_w_down, norm_final_g):
    params = dict(norm_mix_g=norm_mix_g, w_in=w_in, rw_mu=rw_mu, rw_w0=rw_w0, rw_w_up=rw_w_up, rw_a0=rw_a0,
                  rw_a_up=rw_a_up, rw_g_up=rw_g_up, rw_k_k=rw_k_k, rw_k_a=rw_k_a, rw_r_k=rw_r_k,
                  rw_ln_w=rw_ln_w, rw_ln_b=rw_ln_b, mla_q_norm_g=mla_q_norm_g, mla_kv_norm_g=mla_kv_norm_g,
                  mla_w_uq=mla_w_uq, mla_w_uk=mla_w_uk, mla_w_uv=mla_w_uv, p_rwkv=p_rwkv, p_sb=p_sb,
                  p_mla=p_mla, w_out=w_out, norm_ffn_g=norm_ffn_g, ffn_w_a=ffn_w_a, ffn_w_b=ffn_w_b,
                  ffn_conv_w=ffn_conv_w, ffn_conv_b=ffn_conv_b, ffn_w_down=ffn_w_down)
    bp, tp, d = x_prompt.shape
    bs, ts, _ = x_sample.shape
    depth = w_in.shape[0]
    n_pool, page = cache_sb_k.shape[1], cache_sb_k.shape[2]
    past_len = page_table.shape[1] * page
    d_ff = ffn_w_a.shape[2]
    assert page == SB_KB and SB_KVW == LANES

    cst = _constants()
    rope_p = jnp.tile(_rope_table(jnp.arange(tp, dtype=jnp.int32)), (bp, 1))
    rope_s = jnp.tile(_rope_table(past_len + jnp.arange(ts, dtype=jnp.int32)), (bs, 1))
    cache_k = cache_sb_k.reshape(depth, n_pool, page, SB_KVW)
    cache_v = cache_sb_v.reshape(depth, n_pool, page, SB_KVW)
    s0_p = jnp.zeros((bp, RW_HEADS, RW_HD, RW_HD), F32)
    shift0_p = jnp.zeros((bp, RW_COLS), F32)
    conv0_p = jnp.zeros((bp, CONV_W - 1, d_ff), F32)

    xp = x_prompt.reshape(bp * tp, d)
    xs = x_sample.reshape(bs * ts, d)
    p_states, s_states = [], []
    for l in range(depth):
        w = _layer_weights(l, d, params)
        xp, st = _layer(xp, bp, tp, rope_p, w, cst, s0_p, shift0_p, conv0_p, None)
        p_states.append(st)
        past = (l, page_table, cache_k, cache_v, cache_mla_ckv, cache_mla_krope)
        xs, st = _layer(xs, bs, ts, rope_s, w, cst, state_rwkv[l], state_rwkv_shift[l], state_ffn_conv[l], past)
        s_states.append(st)

    g_fin = norm_final_g.reshape(1, d)
    y_prompt = _rmsnorm(xp, g_fin).reshape(bp, tp, d)
    y_sample = _rmsnorm(xs, g_fin).reshape(bs, ts, d)

    def stack(states):
        s_new, shift, k, v, c, r, conv = (jnp.stack(z, axis=0) for z in zip(*states))
        return k, v, c, r, s_new, shift, conv

    return (y_prompt, y_sample) + stack(p_states) + stack(s_states)
```

```python
import functools
import math

import numpy as np
import jax
import jax.numpy as jnp
from jax import lax
from jax.experimental import pallas as pl
from jax.experimental.pallas import tpu as pltpu

F32 = jnp.float32
BF16 = jnp.bfloat16
HI = lax.Precision.HIGHEST

RW_HEADS, RW_HD = 8, 64
RW_W = RW_HEADS * RW_HD
W_LORA, A_LORA, G_LORA = 64, 64, 160
RW_COLS = 3 * RW_W + W_LORA + A_LORA + G_LORA
RW_GN_EPS = 64e-5
SB_HEADS, SB_KV, SB_HD = 8, 2, 64
SB_GROUP = SB_HEADS // SB_KV
SB_Q = SB_HEADS * SB_HD
SB_KVW = SB_KV * SB_HD
MLA_HEADS = 8
Q_LORA, KV_LORA, NOPE, ROPE, V_DIM = 256, 256, 64, 32, 64
MLA_SCALE = (NOPE + ROPE) ** -0.5
ROPE_BASE = 10000.0
QC = KV_LORA + ROPE
NORM_EPS = 1e-6
CONV_W = 3

VMEM_LIMIT = 56 * 1024 * 1024
LANES = 128


def _cparams(sem):
    return pltpu.CompilerParams(dimension_semantics=sem, vmem_limit_bytes=VMEM_LIMIT)


def _dot(a, b):
    return jnp.dot(a, b, preferred_element_type=F32)


def _dot_hi(a, b):
    return jnp.dot(a, b, preferred_element_type=F32, precision=HI)


def _dot_nt(a, b):
    return lax.dot_general(a, b, (((1,), (1,)), ((), ())), preferred_element_type=F32)


def _split_hi_lo(x):
    hi = x.astype(BF16)
    lo = (x - hi.astype(F32)).astype(BF16)
    return hi, lo


def _resid(x):
    return x - x.astype(BF16).astype(F32)


def _sigmoid(x):
    return 1.0 / (1.0 + jnp.exp(-x))


def _softplus(x):
    return jnp.maximum(x, 0.0) + jnp.log1p(jnp.exp(-jnp.abs(x)))


def _full(shape):
    n = len(shape)
    return pl.BlockSpec(shape, lambda *_: (0,) * n)


def _row_tile(m, cap=256):
    t = min(cap, m)
    assert m % t == 0
    return t


def _inproj_kernel(x_ref, g_ref, rope_ref, gq_ref, gkv_ref,
                   w_rw, w_sbq, w_sbk, w_sbv, w_cq, w_ckv, w_kr, w_gate,
                   o_rw, o_sbq, o_sbk, o_sbv, o_cq, o_ckv, o_kr, o_gate):
    x = x_ref[...]
    ms = jnp.mean(x * x, axis=-1, keepdims=True)
    h = (x * lax.rsqrt(ms + NORM_EPS) * g_ref[...]).astype(BF16)
    o_rw[...] = _dot(h, w_rw[...])
    o_sbq[...] = _dot(h, w_sbq[...])
    o_sbk[...] = _dot(h, w_sbk[...])
    o_sbv[...] = _dot(h, w_sbv[...])
    o_gate[...] = _dot(h, w_gate[...])

    def lat_norm(c, g):
        return c * lax.rsqrt(jnp.mean(c * c, axis=-1, keepdims=True) + NORM_EPS) * g

    o_cq[...] = lat_norm(_dot(h, w_cq[...]), gq_ref[...])
    o_ckv[...] = lat_norm(_dot(h, w_ckv[...]), gkv_ref[...])
    kr2 = _dot(h, w_kr[...])
    rope = rope_ref[...]
    o_kr[...] = kr2[:, :ROPE] * rope[:, :ROPE] + kr2[:, ROPE:] * rope[:, ROPE:]


def _inproj(x2, g, rope, gq, gkv, ws):
    m, d = x2.shape
    tm = _row_tile(m)
    widths = (RW_COLS, SB_Q, SB_KVW, SB_KVW, Q_LORA, KV_LORA, ROPE, 3 * d)
    row = lambda n: pl.BlockSpec((tm, n), lambda i: (i, 0))
    in_specs = [row(d), _full((1, d)), row(2 * ROPE), _full((1, Q_LORA)), _full((1, KV_LORA))]
    in_specs += [_full(w.shape) for w in ws]
    return pl.pallas_call(
        _inproj_kernel,
        grid=(m // tm,),
        in_specs=in_specs,
        out_specs=[row(n) for n in widths],
        out_shape=[jax.ShapeDtypeStruct((m, n), F32) for n in widths],
        compiler_params=_cparams(("parallel",)),
    )(x2, g, rope, gq, gkv, *ws)


def _rwkv_pre_kernel(c_ref, p_ref, mu_ref, w0_ref, wup_ref, a0_ref, aup_ref, gup_ref,
                     kk_ref, ka_ref, rk_ref, seg_ref,
                     o_r, o_w, o_k, o_v, o_kkn, o_b, o_g, o_bonus):
    c = c_ref[...]
    xs = c + (p_ref[...] - c) * mu_ref[...]
    r = xs[:, 0:RW_W]
    k = xs[:, RW_W:2 * RW_W]
    v = xs[:, 2 * RW_W:3 * RW_W]
    o0 = 3 * RW_W
    wd = xs[:, o0:o0 + W_LORA]
    ad = xs[:, o0 + W_LORA:o0 + W_LORA + A_LORA]
    gd = xs[:, o0 + W_LORA + A_LORA:RW_COLS]
    wl = w0_ref[...] + _dot_hi(jnp.tanh(wd), wup_ref[...])
    w = -_softplus(-wl) - 0.5
    decay = jnp.exp(-jnp.exp(w))
    a = _sigmoid(a0_ref[...] + _dot_hi(ad, aup_ref[...]))
    g = _dot_hi(_sigmoid(gd), gup_ref[...])
    seg = seg_ref[...]
    kk = k * kk_ref[...]
    kk = kk * lax.rsqrt(jnp.maximum(_dot_hi(kk * kk, seg), 1e-24))
    k2 = k * (1.0 + (a - 1.0) * ka_ref[...])
    o_r[...] = r
    o_w[...] = decay
    o_k[...] = k2
    o_v[...] = v
    o_kkn[...] = -kk
    o_b[...] = kk * a
    o_g[...] = g
    o_bonus[...] = _dot_hi(r * k2 * rk_ref[...], seg) * v


def _rwkv_pre(c2, p2, mu, w0, wup, a0, aup, gup, kk, ka, rk, seg):
    m = c2.shape[0]
    tm = _row_tile(m)
    rowc = pl.BlockSpec((tm, RW_COLS), lambda i: (i, 0))
    roww = pl.BlockSpec((tm, RW_W), lambda i: (i, 0))
    vec = _full((1, RW_W))
    return pl.pallas_call(
        _rwkv_pre_kernel,
        grid=(m // tm,),
        in_specs=[rowc, rowc, _full((1, RW_COLS)), vec, _full(wup.shape), vec, _full(aup.shape),
                  _full(gup.shape), vec, vec, vec, _full(seg.shape)],
        out_specs=[roww] * 8,
        out_shape=[jax.ShapeDtypeStruct((m, RW_W), F32)] * 8,
        compiler_params=_cparams(("parallel",)),
    )(c2, p2, mu, w0, wup, a0, aup, gup, kk, ka, rk, seg)


RW_PAIRS = RW_HEADS // 2
RW_SUB = 64
RW_SPLIT = 2


def _rwkv_scan_kernel(bb, tc, r_ref, w_ref, k_ref, v_ref, kkn_ref, b_ref, s0_ref, eye_ref, sum1_ref,
                      sum1h_ref, sum2_ref, o_ref, sfin_ref, s_scr, o_scr):
    ci = pl.program_id(1)
    groups = [(bi, hp) for bi in range(bb) for hp in range(RW_PAIRS)]
    per = len(groups) // RW_SPLIT

    @pl.when(ci == 0)
    def _():
        o_scr[...] = jnp.zeros_like(o_scr)
        for gi, (bi, hp) in enumerate(groups):
            s_scr[gi] = jnp.concatenate([s0_ref[bi, 2 * hp], s0_ref[bi, 2 * hp + 1]], axis=1)

    eye2 = eye_ref[...]
    sum1 = sum1_ref[...]
    sum1h = sum1h_ref[...]
    sum2 = sum2_ref[...]
    lane = lax.broadcasted_iota(jnp.int32, (RW_HD, LANES), 1) % RW_HD
    sub = min(RW_SUB, tc)
    rows = lambda j: slice(j * RW_HD, (j + 1) * RW_HD)

    def step8(t0, t8, _):
        base = pl.multiple_of(t0 + t8 * 8, 8)
        for part in range(RW_SPLIT):
            gs = list(range(part * per, (part + 1) * per))
            tiles = {}
            for gi in gs:
                bi, hp = groups[gi]
                ls = pl.ds(hp * LANES, LANES)
                tiles[gi] = tuple(ref[bi, pl.ds(base, 8), ls]
                                  for ref in (kkn_ref, v_ref, w_ref, b_ref, k_ref, r_ref))

            def emit_o(ob, step):
                for j, gi in enumerate(gs):
                    o_scr[gi] = jnp.where(lane == t8 * 8 + step, ob[rows(j)], o_scr[gi])

            t2 = None
            for i in range(9):
                row = lambda x: x[i:i + 1, :]
                if i == 0:
                    vb = _dot(jnp.concatenate([eye2 * row(tiles[gi][1]) for gi in gs], axis=0), sum1h)
                elif i < 8:
                    lhs = jnp.concatenate([jnp.concatenate([eye2 * row(tiles[gi][1]), t2[j]], axis=1)
                                           for j, gi in enumerate(gs)], axis=0)
                    both = _dot(lhs, sum2)
                    vb = both[:, :LANES]
                    emit_o(both[:, LANES:], i - 1)
                else:
                    emit_o(_dot(jnp.concatenate(t2, axis=0), sum1h), 7)
                    break
                lhs = []
                for gi in gs:
                    t1 = s_scr[gi] * row(tiles[gi][0])
                    lhs.append(jnp.concatenate([t1, _resid(t1)], axis=1))
                sa = _dot(jnp.concatenate(lhs, axis=0), sum1)
                t2 = []
                for j, gi in enumerate(gs):
                    _, _, w8, b8, k8, r8 = tiles[gi]
                    s = s_scr[gi] * row(w8) + sa[rows(j)] * row(b8) + vb[rows(j)] * row(k8)
                    s_scr[gi] = s
                    t2.append(s * row(r8))
        return 0

    def chunk(c, _):
        t0 = pl.multiple_of(c * sub, sub)
        lax.fori_loop(0, sub // 8, functools.partial(step8, t0), 0)
        for gi, (bi, hp) in enumerate(groups):
            ot = o_scr[gi].T
            o_ref[bi, pl.ds(t0, sub), hp * LANES:hp * LANES + RW_HD] = ot[0:sub]
            o_ref[bi, pl.ds(t0, sub), hp * LANES + RW_HD:(hp + 1) * LANES] = ot[RW_HD:RW_HD + sub]
        return 0

    lax.fori_loop(0, tc // sub, chunk, 0)

    @pl.when(ci == pl.num_programs(1) - 1)
    def _():
        for gi, (bi, hp) in enumerate(groups):
            s = s_scr[gi]
            sfin_ref[bi, 2 * hp] = s[:, :RW_HD]
            sfin_ref[bi, 2 * hp + 1] = s[:, RW_HD:]


def _rwkv_scan(r, w, k, v, kkn, b, s0, consts):
    bsz, t, _ = r.shape
    bb = 4
    tc = min(t, 256)
    assert bsz % bb == 0 and t % tc == 0 and (tc % RW_SUB == 0 or tc < RW_SUB) and tc % 8 == 0
    seq = pl.BlockSpec((bb, tc, RW_W), lambda i, c: (i, c, 0))
    st = pl.BlockSpec((bb, RW_HEADS, RW_HD, RW_HD), lambda i, c: (i, 0, 0, 0))
    return pl.pallas_call(
        functools.partial(_rwkv_scan_kernel, bb, tc),
        grid=(bsz // bb, t // tc),
        in_specs=[seq] * 6 + [st] + [_full(c.shape) for c in consts],
        out_specs=[seq, st],
        out_shape=[jax.ShapeDtypeStruct((bsz, t, RW_W), F32),
                   jax.ShapeDtypeStruct((bsz, RW_HEADS, RW_HD, RW_HD), F32)],
        scratch_shapes=[pltpu.VMEM((bb * RW_PAIRS, RW_HD, LANES), F32),
                        pltpu.VMEM((bb * RW_PAIRS, RW_HD, LANES), F32)],
        compiler_params=_cparams(("parallel", "arbitrary")),
    )(r, w, k, v, kkn, b, s0, *consts)


def _rwkv_post_kernel(o_ref, bonus_ref, g_ref, lnw_ref, lnb_ref, seg_ref, out_ref):
    o = o_ref[...]
    seg = seg_ref[...]
    mean = _dot_hi(o, seg) * (1.0 / RW_HD)
    d = o - mean
    var = _dot_hi(d * d, seg) * (1.0 / RW_HD)
    y = d * lax.rsqrt(var + RW_GN_EPS) * lnw_ref[...] + lnb_ref[...]
    out_ref[...] = (y + bonus_ref[...]) * g_ref[...]


def _rwkv_post(o2, bonus, g, lnw, lnb, seg):
    m = o2.shape[0]
    tm = _row_tile(m)
    roww = pl.BlockSpec((tm, RW_W), lambda i: (i, 0))
    vec = _full((1, RW_W))
    return pl.pallas_call(
        _rwkv_post_kernel,
        grid=(m // tm,),
        in_specs=[roww, roww, roww, vec, vec, _full(seg.shape)],
        out_specs=roww,
        out_shape=jax.ShapeDtypeStruct((m, RW_W), F32),
        compiler_params=_cparams(("parallel",)),
    )(o2, bonus, g, lnw, lnb, seg)


SB_KB = 128


def _sb_block(q, kb, vb, tri, run, acc, mask):
    z = _dot_nt(q, kb.astype(BF16))
    l1m = -_softplus(z)
    if mask is not None:
        l1m = jnp.where(mask, l1m, 0.0)
    hi, lo = _split_hi_lo(l1m)
    after = _dot(hi, tri) + _dot(lo, tri)
    a = jnp.exp(z + l1m + after + run)
    if mask is not None:
        a = jnp.where(mask, a, 0.0)
    acc = acc + _dot(a.astype(BF16), vb.astype(BF16))
    run = run + after[:, 0:1] + l1m[:, 0:1]
    return run, acc


def _sb_prompt_kernel(tq, q_ref, k_ref, v_ref, tri_ref, o_ref, q_scr, run_scr, acc_scr):
    qi = pl.program_id(1)
    j = pl.program_id(2)
    rows = SB_GROUP * tq
    nsub = tq // SB_KB

    @pl.when(j == 0)
    def _():
        for kvh in range(SB_KV):
            for g in range(SB_GROUP):
                h = kvh * SB_GROUP + g
                q_scr[kvh, g * tq:(g + 1) * tq, :] = (q_ref[:, h * SB_HD:(h + 1) * SB_HD] * SB_HD ** -0.5).astype(BF16)
        run_scr[...] = jnp.zeros_like(run_scr)
        acc_scr[...] = jnp.zeros_like(acc_scr)

    def sweep(diag):
        tri = tri_ref[...]
        for kvh in range(SB_KV):
            q = q_scr[kvh]
            run = run_scr[kvh]
            acc = acc_scr[kvh]
            for sb in reversed(range(nsub)):
                kb = k_ref[sb * SB_KB:(sb + 1) * SB_KB, kvh * SB_HD:(kvh + 1) * SB_HD]
                vb = v_ref[sb * SB_KB:(sb + 1) * SB_KB, kvh * SB_HD:(kvh + 1) * SB_HD]
                mask = None
                if diag:
                    qpos = lax.broadcasted_iota(jnp.int32, (rows, SB_KB), 0) % tq
                    kpos = lax.broadcasted_iota(jnp.int32, (rows, SB_KB), 1) + sb * SB_KB
                    mask = kpos < qpos
                run, acc = _sb_block(q, kb, vb, tri, run, acc, mask)
            run_scr[kvh] = run
            acc_scr[kvh] = acc

    @pl.when(j == 0)
    def _():
        sweep(True)

    @pl.when(jnp.logical_and(j > 0, j <= qi))
    def _():
        sweep(False)

    @pl.when(j == qi)
    def _():
        for kvh in range(SB_KV):
            for g in range(SB_GROUP):
                h = kvh * SB_GROUP + g
                o_ref[:, h * SB_HD:(h + 1) * SB_HD] = acc_scr[kvh, g * tq:(g + 1) * tq, :]


def _sb_prompt(q, k, v, tri):
    bsz, t, _ = q.shape
    tq = min(256, t)
    nq = t // tq
    rows = SB_GROUP * tq
    kv_spec = pl.BlockSpec((None, tq, SB_KVW), lambda b, i, j: (b, jnp.maximum(i - j, 0), 0))
    return pl.pallas_call(
        functools.partial(_sb_prompt_kernel, tq),
        grid=(bsz, nq, nq),
        in_specs=[pl.BlockSpec((None, tq, SB_Q), lambda b, i, j: (b, i, 0)), kv_spec, kv_spec,
                  _full(tri.shape)],
        out_specs=pl.BlockSpec((None, tq, SB_Q), lambda b, i, j: (b, i, 0)),
        out_shape=jax.ShapeDtypeStruct((bsz, t, SB_Q), F32),
        scratch_shapes=[pltpu.VMEM((SB_KV, rows, SB_HD), BF16),
                        pltpu.VMEM((SB_KV, rows, 1), F32),
                        pltpu.VMEM((SB_KV, rows, SB_HD), F32)],
        compiler_params=_cparams(("parallel", "parallel", "arbitrary")),
    )(q, k, v, tri)


N_PAGES_STEP = 16


def _pad_keys(x, n):
    return jnp.concatenate([x, jnp.zeros((n - x.shape[0], x.shape[1]), x.dtype)], axis=0)


def _sb_sample_kernel(t_new, n_pg, *refs):
    q_ref, kn_ref, vn_ref, tri_ref = refs[1:5]
    kt_refs = refs[5:5 + n_pg]
    vt_refs = refs[5 + n_pg:5 + 2 * n_pg]
    o_ref, run_scr, acc_scr = refs[5 + 2 * n_pg:]
    j = pl.program_id(1)
    rows = q_ref.shape[0]
    tri2 = tri_ref[...]
    q = q_ref[...]

    def later_sums(ls):
        lhs = jnp.concatenate([jnp.concatenate([l, _resid(l)], axis=1) for l in ls], axis=0)
        after = _dot(lhs, tri2)
        return [after[i * rows:(i + 1) * rows] for i in range(len(ls))]

    @pl.when(j == 0)
    def _():
        kn = _pad_keys(kn_ref[...], SB_KB)
        vn = _pad_keys(vn_ref[...], SB_KB)
        qpos = lax.broadcasted_iota(jnp.int32, (rows, SB_KB), 0) % t_new
        kpos = lax.broadcasted_iota(jnp.int32, (rows, SB_KB), 1)
        mask = kpos < qpos
        z = _dot_nt(q, kn)
        l1m = jnp.where(mask, -_softplus(z), 0.0)
        after = later_sums([l1m])[0]
        a = jnp.where(mask, jnp.exp(z + l1m + after), 0.0)
        acc_scr[...] = _dot(a, vn)
        run_scr[...] = after[:, 0:1] + l1m[:, 0:1]

    zs = [_dot(q, kt_refs[p][...]) for p in range(n_pg)]
    ls = [-_softplus(z) for z in zs]
    afters = later_sums(ls)
    run = run_scr[...]
    acc = acc_scr[...]
    for p in range(n_pg):
        a = jnp.exp(zs[p] + ls[p] + afters[p] + run)
        acc = acc + _dot_nt(a, vt_refs[p][...])
        run = run + afters[p][:, 0:1] + ls[p][:, 0:1]
    run_scr[...] = run
    acc_scr[...] = acc

    @pl.when(j == pl.num_programs(1) - 1)
    def _():
        o_ref[...] = acc


def _sb_sample(layer, page_table, qpad, kn, vn, cache_kt, cache_vt, tri2, t_new):
    bsz, rows, _ = qpad.shape
    n_pages = page_table.shape[1]
    n_pg = min(N_PAGES_STEP, n_pages)
    assert n_pages % n_pg == 0 and cache_kt.shape[3] == SB_KB
    steps = n_pages // n_pg

    def page_spec(p):
        return pl.BlockSpec((None, None, SB_KVW, SB_KB),
                            lambda b, j, pt: (layer, pt[b, n_pages - 1 - (j * n_pg + p)], 0, 0))

    per_b = lambda n: pl.BlockSpec((None, n, LANES), lambda b, j, pt: (b, 0, 0))
    grid_spec = pltpu.PrefetchScalarGridSpec(
        num_scalar_prefetch=1,
        grid=(bsz, steps),
        in_specs=[per_b(rows), per_b(t_new), per_b(t_new), pl.BlockSpec(tri2.shape, lambda b, j, pt: (0, 0))]
        + [page_spec(p) for p in range(n_pg)] * 2,
        out_specs=per_b(rows),
        scratch_shapes=[pltpu.VMEM((rows, 1), F32), pltpu.VMEM((rows, LANES), F32)],
    )
    return pl.pallas_call(
        functools.partial(_sb_sample_kernel, t_new, n_pg),
        grid_spec=grid_spec,
        out_shape=jax.ShapeDtypeStruct((bsz, rows, LANES), F32),
        compiler_params=_cparams(("parallel", "arbitrary")),
    )(page_table, qpad, kn, vn, tri2, *([cache_kt] * n_pg), *([cache_vt] * n_pg))


def _mla_qprep_kernel(cq_ref, rope_ref, wqn_ref, wqr_ref, wukt_ref, o_ref):
    cq = cq_ref[...].astype(BF16)
    rope = rope_ref[...]
    for h in range(MLA_HEADS):
        qn = _dot(cq, wqn_ref[h])
        qr2 = _dot(cq, wqr_ref[h])
        o_ref[h, :, 0:KV_LORA] = _dot(qn.astype(BF16), wukt_ref[h]) * MLA_SCALE
        o_ref[h, :, KV_LORA:QC] = (qr2[:, :ROPE] * rope[:, :ROPE] + qr2[:, ROPE:] * rope[:, ROPE:]) * MLA_SCALE


def _mla_qprep(cq, rope, wqn, wqr, wukt):
    m = cq.shape[0]
    tm = _row_tile(m)
    return pl.pallas_call(
        _mla_qprep_kernel,
        grid=(m // tm,),
        in_specs=[pl.BlockSpec((tm, Q_LORA), lambda i: (i, 0)), pl.BlockSpec((tm, 2 * ROPE), lambda i: (i, 0)),
                  _full(wqn.shape), _full(wqr.shape), _full(wukt.shape)],
        out_specs=pl.BlockSpec((MLA_HEADS, tm, QC), lambda i: (0, i, 0)),
        out_shape=jax.ShapeDtypeStruct((MLA_HEADS, m, QC), F32),
        compiler_params=_cparams(("parallel",)),
    )(cq, rope, wqn, wqr, wukt)


def _mla_prompt_kernel(tq, q_ref, ckv_ref, kr_ref, o_ref, ql_scr, qr_scr, m_scr, l_scr, acc_scr):
    qi = pl.program_id(1)
    j = pl.program_id(2)

    @pl.when(j == 0)
    def _():
        for h in range(MLA_HEADS):
            ql_scr[h] = q_ref[h, :, 0:KV_LORA].astype(BF16)
            qr_scr[h] = q_ref[h, :, KV_LORA:QC].astype(BF16)
        m_scr[...] = jnp.full_like(m_scr, -jnp.inf)
        l_scr[...] = jnp.zeros_like(l_scr)
        acc_scr[...] = jnp.zeros_like(acc_scr)

    def sweep(diag):
        cb = ckv_ref[...].astype(BF16)
        rb = kr_ref[...].astype(BF16)
        if diag:
            mask = (lax.broadcasted_iota(jnp.int32, (tq, tq), 1) <= lax.broadcasted_iota(jnp.int32, (tq, tq), 0))
        for h in range(MLA_HEADS):
            s = _dot_nt(ql_scr[h], cb) + _dot_nt(qr_scr[h], rb)
            if diag:
                s = jnp.where(mask, s, -jnp.inf)
            m_prev = m_scr[h]
            m_new = jnp.maximum(m_prev, jnp.max(s, axis=-1, keepdims=True))
            alpha = jnp.exp(m_prev - m_new)
            p = jnp.exp(s - m_new)
            l_scr[h] = alpha * l_scr[h] + jnp.sum(p, axis=-1, keepdims=True)
            acc_scr[h] = alpha * acc_scr[h] + _dot(p.astype(BF16), cb)
            m_scr[h] = m_new

    @pl.when(j == 0)
    def _():
        sweep(True)

    @pl.when(jnp.logical_and(j > 0, j <= qi))
    def _():
        sweep(False)

    @pl.when(j == qi)
    def _():
        for h in range(MLA_HEADS):
            o_ref[h] = acc_scr[h] * (1.0 / l_scr[h])


def _mla_prompt(qcat, ckv, kr, bsz, t):
    tq = min(256, t)
    nq = t // tq
    kblk = lambda b, i, j: (b * nq + jnp.maximum(i - j, 0), 0)
    return pl.pallas_call(
        functools.partial(_mla_prompt_kernel, tq),
        grid=(bsz, nq, nq),
        in_specs=[pl.BlockSpec((MLA_HEADS, tq, QC), lambda b, i, j: (0, b * nq + i, 0)),
                  pl.BlockSpec((tq, KV_LORA), kblk), pl.BlockSpec((tq, ROPE), kblk)],
        out_specs=pl.BlockSpec((MLA_HEADS, tq, KV_LORA), lambda b, i, j: (0, b * nq + i, 0)),
        out_shape=jax.ShapeDtypeStruct((MLA_HEADS, bsz * t, KV_LORA), F32),
        scratch_shapes=[pltpu.VMEM((MLA_HEADS, tq, KV_LORA), BF16), pltpu.VMEM((MLA_HEADS, tq, ROPE), BF16),
                        pltpu.VMEM((MLA_HEADS, tq, 1), F32), pltpu.VMEM((MLA_HEADS, tq, 1), F32),
                        pltpu.VMEM((MLA_HEADS, tq, KV_LORA), F32)],
        compiler_params=_cparams(("parallel", "parallel", "arbitrary")),
    )(qcat, ckv, kr)


def _mla_sample_kernel(t_new, n_pg, *refs):
    q_ref, cn_ref, rn_ref = refs[1:4]
    c_refs = refs[4:4 + n_pg]
    rt_refs = refs[4 + n_pg:4 + 2 * n_pg]
    o_ref, m_scr, l_scr, acc_scr = refs[4 + 2 * n_pg:]
    j = pl.program_id(1)
    rows = MLA_HEADS * t_new
    q = q_ref[...].reshape(rows, QC)
    ql = q[:, 0:KV_LORA]
    qr = q[:, KV_LORA:QC]

    @pl.when(j == 0)
    def _():
        cn = _pad_keys(cn_ref[...], SB_KB)
        rn = _pad_keys(rn_ref[...], SB_KB)
        qpos = lax.broadcasted_iota(jnp.int32, (rows, SB_KB), 0) % t_new
        kpos = lax.broadcasted_iota(jnp.int32, (rows, SB_KB), 1)
        s = jnp.where(kpos <= qpos, _dot_nt(ql, cn) + _dot_nt(qr, rn), -jnp.inf)
        m0 = jnp.max(s, axis=-1, keepdims=True)
        p = jnp.exp(s - m0)
        m_scr[...] = m0
        l_scr[...] = jnp.sum(p, axis=-1, keepdims=True)
        acc_scr[...] = _dot(p, cn)

    s = jnp.concatenate([_dot_nt(ql, c_refs[p][...]) + _dot(qr, rt_refs[p][...]) for p in range(n_pg)], axis=1)
    m_prev = m_scr[...]
    m_new = jnp.maximum(m_prev, jnp.max(s, axis=-1, keepdims=True))
    alpha = jnp.exp(m_prev - m_new)
    pr = jnp.exp(s - m_new)
    l_new = alpha * l_scr[...] + jnp.sum(pr, axis=-1, keepdims=True)
    acc = alpha * acc_scr[...]
    page = c_refs[0].shape[0]
    for p in range(n_pg):
        acc = acc + _dot(pr[:, p * page:(p + 1) * page], c_refs[p][...])
    m_scr[...] = m_new
    l_scr[...] = l_new
    acc_scr[...] = acc

    @pl.when(j == pl.num_programs(1) - 1)
    def _():
        o_ref[...] = (acc * (1.0 / l_new)).reshape(MLA_HEADS, t_new, KV_LORA)


def _mla_sample(layer, page_table, qcat, cn, rn, cache_c, cache_rt, t_new):
    bsz = page_table.shape[0]
    n_pages = page_table.shape[1]
    n_pg = min(N_PAGES_STEP, n_pages)
    assert n_pages % n_pg == 0
    steps = n_pages // n_pg
    page = cache_c.shape[2]
    rows = MLA_HEADS * t_new
    pidx = lambda p: (lambda b, j, pt: (layer, pt[b, j * n_pg + p], 0, 0))
    grid_spec = pltpu.PrefetchScalarGridSpec(
        num_scalar_prefetch=1,
        grid=(bsz, steps),
        in_specs=[pl.BlockSpec((MLA_HEADS, t_new, QC), lambda b, j, pt: (0, b, 0)),
                  pl.BlockSpec((None, t_new, KV_LORA), lambda b, j, pt: (b, 0, 0)),
                  pl.BlockSpec((None, t_new, ROPE), lambda b, j, pt: (b, 0, 0))]
        + [pl.BlockSpec((None, None, page, KV_LORA), pidx(p)) for p in range(n_pg)]
        + [pl.BlockSpec((None, None, ROPE, page), pidx(p)) for p in range(n_pg)],
        out_specs=pl.BlockSpec((MLA_HEADS, t_new, KV_LORA), lambda b, j, pt: (0, b, 0)),
        scratch_shapes=[pltpu.VMEM((rows, 1), F32), pltpu.VMEM((rows, 1), F32),
                        pltpu.VMEM((rows, KV_LORA), F32)],
    )
    return pl.pallas_call(
        functools.partial(_mla_sample_kernel, t_new, n_pg),
        grid_spec=grid_spec,
        out_shape=jax.ShapeDtypeStruct((MLA_HEADS, bsz * t_new, KV_LORA), F32),
        compiler_params=_cparams(("parallel", "arbitrary")),
    )(page_table, qcat, cn, rn, *([cache_c] * n_pg), *([cache_rt] * n_pg))


def _merge_kernel(x_ref, gate_ref, orw_ref, osb_ref, olat_ref, wuv_ref, prw_ref, psb_ref, pmla_ref,
                  wout_ref, y_ref):
    d = x_ref.shape[1]
    gate = gate_ref[...]
    o_mla = jnp.concatenate(
        [_dot(olat_ref[h].astype(BF16), wuv_ref[h]) for h in range(MLA_HEADS)], axis=1)
    merged = (_sigmoid(gate[:, 0:d]) * _dot(orw_ref[...].astype(BF16), prw_ref[...])
              + _sigmoid(gate[:, d:2 * d]) * _dot(osb_ref[...].astype(BF16), psb_ref[...])
              + _sigmoid(gate[:, 2 * d:3 * d]) * _dot(o_mla.astype(BF16), pmla_ref[...]))
    y_ref[...] = x_ref[...] + _dot(merged.astype(BF16), wout_ref[...])


def _merge(x2, gate, o_rw, o_sb, o_lat, wuv, prw, psb, pmla, wout):
    m, d = x2.shape
    tm = _row_tile(m)
    row = lambda n: pl.BlockSpec((tm, n), lambda i: (i, 0))
    return pl.pallas_call(
        _merge_kernel,
        grid=(m // tm,),
        in_specs=[row(d), row(3 * d), row(RW_W), row(SB_Q),
                  pl.BlockSpec((MLA_HEADS, tm, KV_LORA), lambda i: (0, i, 0)),
                  _full(wuv.shape), _full(prw.shape), _full(psb.shape), _full(pmla.shape), _full(wout.shape)],
        out_specs=row(d),
        out_shape=jax.ShapeDtypeStruct((m, d), F32),
        compiler_params=_cparams(("parallel",)),
    )(x2, gate, o_rw, o_sb, o_lat, wuv, prw, psb, pmla, wout)


def _ffn_up_kernel(x_ref, g_ref, wa_ref, wb_ref, u_ref, gin_ref):
    x = x_ref[...]
    ms = jnp.mean(x * x, axis=-1, keepdims=True)
    h = (x * lax.rsqrt(ms + NORM_EPS) * g_ref[...]).astype(BF16)
    u_ref[...] = _dot(h, wa_ref[...])
    gin_ref[...] = _dot(h, wb_ref[...])


def _ffn_up(x2, g, wa, wb):
    m, d = x2.shape
    f = wa.shape[1]
    tm = _row_tile(m)
    row = lambda n: pl.BlockSpec((tm, n), lambda i: (i, 0))
    return pl.pallas_call(
        _ffn_up_kernel,
        grid=(m // tm,),
        in_specs=[row(d), _full((1, d)), _full(wa.shape), _full(wb.shape)],
        out_specs=[row(f), row(f)],
        out_shape=[jax.ShapeDtypeStruct((m, f), F32)] * 2,
        compiler_params=_cparams(("parallel",)),
    )(x2, g, wa, wb)


def _gelu_tanh(x):
    c = math.sqrt(2.0 / math.pi)
    return x * (0.5 * (1.0 + jnp.tanh(c * (x + 0.044715 * (x * x * x)))))


def _ffn_down_kernel(x_ref, u_ref, u1_ref, u2_ref, gin_ref, cw_ref, cb_ref, wd_ref, y_ref):
    cw = cw_ref[...]
    conv = cb_ref[...] + cw[0:1] * u2_ref[...] + cw[1:2] * u1_ref[...] + cw[2:3] * u_ref[...]
    act = (_gelu_tanh(conv) * gin_ref[...]).astype(BF16)
    y_ref[...] = x_ref[...] + _dot(act, wd_ref[...])


def _ffn_down(x2, u, u1, u2, gin, cw, cb, wd):
    m, d = x2.shape
    f = u.shape[1]
    tm = _row_tile(m)
    row = lambda n: pl.BlockSpec((tm, n), lambda i: (i, 0))
    return pl.pallas_call(
        _ffn_down_kernel,
        grid=(m // tm,),
        in_specs=[row(d), row(f), row(f), row(f), row(f), _full(cw.shape), _full((1, f)), _full(wd.shape)],
        out_specs=row(d),
        out_shape=jax.ShapeDtypeStruct((m, d), F32),
        compiler_params=_cparams(("parallel",)),
    )(x2, u, u1, u2, gin, cw, cb, wd)


def _rmsnorm_kernel(x_ref, g_ref, y_ref):
    x = x_ref[...]
    y_ref[...] = x * lax.rsqrt(jnp.mean(x * x, axis=-1, keepdims=True) + NORM_EPS) * g_ref[...]


def _rmsnorm(x2, g):
    m, d = x2.shape
    tm = _row_tile(m, 512)
    return pl.pallas_call(
        _rmsnorm_kernel,
        grid=(m // tm,),
        in_specs=[pl.BlockSpec((tm, d), lambda i: (i, 0)), _full((1, d))],
        out_specs=pl.BlockSpec((tm, d), lambda i: (i, 0)),
        out_shape=jax.ShapeDtypeStruct((m, d), F32),
        compiler_params=_cparams(("parallel",)),
    )(x2, g)


def _constants():
    i2 = np.arange(2 * LANES)
    lanes = np.arange(LANES)
    head = lambda x: (x % LANES) // RW_HD
    i5 = np.arange(RW_W)
    kb = np.arange(SB_KB)
    tri = kb[:, None] > kb[None, :]
    sum1 = head(i2)[:, None] == head(lanes)[None, :]
    return dict(
        seg=jnp.asarray((i5[:, None] // RW_HD) == (i5[None, :] // RW_HD), F32),
        scan=(jnp.asarray(np.concatenate([np.eye(RW_HD), np.eye(RW_HD)], axis=1), F32),
              jnp.asarray(sum1, F32), jnp.asarray(sum1[:LANES], F32),
              jnp.asarray((i2[:, None] // RW_HD) == (i2[None, :] // RW_HD), F32)),
        tri=jnp.asarray(tri, BF16),
        tri2=jnp.asarray(np.concatenate([tri, tri], axis=0), F32),
    )


def _rope_table(pos):
    half = ROPE // 2
    inv = jnp.exp(-math.log(ROPE_BASE) * jnp.arange(half, dtype=F32) / half)
    ang = pos.astype(F32)[:, None] * inv[None, :]
    cos, sin = jnp.cos(ang), jnp.sin(ang)
    return jnp.concatenate([cos, cos, -sin, sin], axis=1)


def _rot_cols(w):
    half = ROPE // 2
    return jnp.concatenate([w[..., half:], w[..., :half]], axis=-1)


def _layer_weights(l, d, p):
    w_in = p['w_in'][l]
    offs = np.cumsum([0, RW_COLS, SB_Q, SB_KVW, SB_KVW, Q_LORA, KV_LORA, ROPE, 3 * d])
    parts = [w_in[:, offs[i]:offs[i + 1]] for i in range(8)]
    parts[6] = jnp.concatenate([parts[6], _rot_cols(parts[6])], axis=1)
    w_uq = p['mla_w_uq'][l]
    wqr = w_uq[:, :, NOPE:]
    row = lambda v: v.reshape(1, -1)
    return dict(
        norm_g=row(p['norm_mix_g'][l]),
        inproj=[w.astype(BF16) for w in parts],
        gq=row(p['mla_q_norm_g'][l]), gkv=row(p['mla_kv_norm_g'][l]),
        mu=row(p['rw_mu'][l]), w0=row(p['rw_w0'][l]), wup=p['rw_w_up'][l], a0=row(p['rw_a0'][l]),
        aup=p['rw_a_up'][l], gup=p['rw_g_up'][l], kk=row(p['rw_k_k'][l]), ka=row(p['rw_k_a'][l]),
        rk=row(p['rw_r_k'][l]), lnw=row(p['rw_ln_w'][l]), lnb=row(p['rw_ln_b'][l]),
        wqn=jnp.transpose(w_uq[:, :, :NOPE], (1, 0, 2)).astype(BF16),
        wqr=jnp.transpose(jnp.concatenate([wqr, _rot_cols(wqr)], axis=-1), (1, 0, 2)).astype(BF16),
        wukt=jnp.transpose(p['mla_w_uk'][l], (1, 2, 0)).astype(BF16),
        wuv=jnp.transpose(p['mla_w_uv'][l], (1, 0, 2)).astype(BF16),
        prw=p['p_rwkv'][l].astype(BF16), psb=p['p_sb'][l].astype(BF16), pmla=p['p_mla'][l].astype(BF16),
        wout=p['w_out'][l].astype(BF16),
        ffn_g=row(p['norm_ffn_g'][l]), wa=p['ffn_w_a'][l].astype(BF16), wb=p['ffn_w_b'][l].astype(BF16),
        cw=p['ffn_conv_w'][l], cb=row(p['ffn_conv_b'][l]), wd=p['ffn_w_down'][l].astype(BF16),
    )


def _shift_rows(x3, first, n):
    assert x3.shape[1] > n
    return jnp.concatenate([first, x3[:, :x3.shape[1] - n]], axis=1)


def _layer(x2, bsz, t, rope, w, cst, rw_s0, rw_shift0, conv_prev, past):
    m, d = x2.shape
    c_rw, sbq, sbk, sbv, cq, ckv, kr, gate = _inproj(x2, w['norm_g'], rope, w['gq'], w['gkv'], w['inproj'])

    c3 = c_rw.reshape(bsz, t, RW_COLS)
    prev = _shift_rows(c3, rw_shift0[:, None, :], 1).reshape(m, RW_COLS)
    r, dec, k2, v, kkn, b, g, bonus = _rwkv_pre(c_rw, prev, w['mu'], w['w0'], w['wup'], w['a0'], w['aup'],
                                                  w['gup'], w['kk'], w['ka'], w['rk'], cst['seg'])
    sq = lambda a: a.reshape(bsz, t, RW_W)
    o, s_new = _rwkv_scan(sq(r), sq(dec), sq(k2), sq(v), sq(kkn), sq(b), rw_s0, cst['scan'])
    o_rw = _rwkv_post(o.reshape(m, RW_W), bonus, g, w['lnw'], w['lnb'], cst['seg'])

    qcat = _mla_qprep(cq, rope, w['wqn'], w['wqr'], w['wukt'])
    if past is None:
        o_sb = _sb_prompt(sbq.reshape(bsz, t, SB_Q), sbk.reshape(bsz, t, SB_KVW), sbv.reshape(bsz, t, SB_KVW),
                          cst['tri']).reshape(m, SB_Q)
        o_lat = _mla_prompt(qcat, ckv, kr, bsz, t)
    else:
        layer, page_table, cache_kt, cache_vt, cache_c, cache_rt = past
        q5 = sbq.reshape(bsz, t, SB_KV, SB_GROUP, SB_HD) * SB_HD ** -0.5
        q5 = jnp.transpose(q5, (0, 2, 3, 1, 4)).reshape(bsz, SB_KV, SB_GROUP * t, SB_HD)
        z = jnp.zeros_like(q5[:, 0])
        qpad = jnp.concatenate([jnp.concatenate([q5[:, 0], z], axis=-1),
                                jnp.concatenate([z, q5[:, 1]], axis=-1)], axis=1)
        o_pad = _sb_sample(layer, page_table, qpad, sbk.reshape(bsz, t, SB_KVW), sbv.reshape(bsz, t, SB_KVW),
                           cache_kt, cache_vt, cst['tri2'], t)
        rows = SB_GROUP * t
        o4 = jnp.stack([o_pad[:, :rows, :SB_HD], o_pad[:, rows:, SB_HD:]], axis=1)
        o_sb = jnp.transpose(o4.reshape(bsz, SB_KV, SB_GROUP, t, SB_HD), (0, 3, 1, 2, 4)).reshape(m, SB_Q)
        o_lat = _mla_sample(layer, page_table, qcat, ckv.reshape(bsz, t, KV_LORA), kr.reshape(bsz, t, ROPE),
                            cache_c, cache_rt, t)

    y = _merge(x2, gate, o_rw, o_sb, o_lat, w['wuv'], w['prw'], w['psb'], w['pmla'], w['wout'])

    u, gin = _ffn_up(y, w['ffn_g'], w['wa'], w['wb'])
    f = u.shape[1]
    u3 = u.reshape(bsz, t, f)
    u1 = _shift_rows(u3, conv_prev[:, 1:], 1).reshape(m, f)
    u2 = _shift_rows(u3, conv_prev, 2).reshape(m, f)
    y = _ffn_down(y, u, u1, u2, gin, w['cw'], w['cb'], w['wd'])
    conv_new = jnp.concatenate([conv_prev, u3], axis=1)[:, t:]
    states = (s_new, c3[:, -1], sbk.reshape(bsz, t, SB_KV, SB_HD), sbv.reshape(bsz, t, SB_KV, SB_HD),
              ckv.reshape(bsz, t, KV_LORA), kr.reshape(bsz, t, ROPE), conv_new)
    return y, states


def kernel(x_prompt, x_sample, cache_sb_k, cache_sb_v, cache_mla_ckv, cache_mla_krope, state_rwkv, state_rwkv_shift, state_ffn_conv, page_table, norm_mix_g, w_in, rw_mu, rw_w0, rw_w_up, rw_a0, rw_a_up, rw_g_up, rw_k_k, rw_k_a, rw_r_k, rw_ln_w, rw_ln_b, mla_q_norm_g, mla_kv_norm_g, mla_w_uq, mla_w_uk, mla_w_uv, p_rwkv, p_sb, p_mla, w_out, norm_ffn_g, ffn_w_a, ffn_w_b, ffn_conv_w, ffn_conv_b, ffn_w_down, norm_final_g):
    params = dict(norm_mix_g=norm_mix_g, w_in=w_in, rw_mu=rw_mu, rw_w0=rw_w0, rw_w_up=rw_w_up, rw_a0=rw_a0,
                  rw_a_up=rw_a_up, rw_g_up=rw_g_up, rw_k_k=rw_k_k, rw_k_a=rw_k_a, rw_r_k=rw_r_k,
                  rw_ln_w=rw_ln_w, rw_ln_b=rw_ln_b, mla_q_norm_g=mla_q_norm_g, mla_kv_norm_g=mla_kv_norm_g,
                  mla_w_uq=mla_w_uq, mla_w_uk=mla_w_uk, mla_w_uv=mla_w_uv, p_rwkv=p_rwkv, p_sb=p_sb,
                  p_mla=p_mla, w_out=w_out, norm_ffn_g=norm_ffn_g, ffn_w_a=ffn_w_a, ffn_w_b=ffn_w_b,
                  ffn_conv_w=ffn_conv_w, ffn_conv_b=ffn_conv_b, ffn_w_down=ffn_w_down)
    bp, tp, d = x_prompt.shape
    bs, ts, _ = x_sample.shape
    depth = w_in.shape[0]
    n_pool, page = cache_sb_k.shape[1], cache_sb_k.shape[2]
    past_len = page_table.shape[1] * page
    d_ff = ffn_w_a.shape[2]
    assert page == SB_KB and SB_KVW == LANES and ts <= SB_KB

    cst = _constants()
    rope_p = jnp.tile(_rope_table(jnp.arange(tp, dtype=jnp.int32)), (bp, 1))
    rope_s = jnp.tile(_rope_table(past_len + jnp.arange(ts, dtype=jnp.int32)), (bs, 1))
    cache_kt = jnp.transpose(cache_sb_k, (0, 1, 3, 4, 2)).reshape(depth, n_pool, SB_KVW, page)
    cache_vt = jnp.transpose(cache_sb_v, (0, 1, 3, 4, 2)).reshape(depth, n_pool, SB_KVW, page)
    cache_rt = jnp.transpose(cache_mla_krope, (0, 1, 3, 2))
    s0_p = jnp.zeros((bp, RW_HEADS, RW_HD, RW_HD), F32)
    shift0_p = jnp.zeros((bp, RW_COLS), F32)
    conv0_p = jnp.zeros((bp, CONV_W - 1, d_ff), F32)

    xp = x_prompt.reshape(bp * tp, d)
    xs = x_sample.reshape(bs * ts, d)
    p_states, s_states = [], []
    for l in range(depth):
        w = _layer_weights(l, d, params)
        xp, st = _layer(xp, bp, tp, rope_p, w, cst, s0_p, shift0_p, conv0_p, None)
        p_states.append(st)
        past = (l, page_table, cache_kt, cache_vt, cache_mla_ckv, cache_rt)
        xs, st = _layer(xs, bs, ts, rope_s, w, cst, state_rwkv[l], state_rwkv_shift[l], state_ffn_conv[l], past)
        s_states.append(st)

    g_fin = norm_final_g.reshape(1, d)
    y_prompt = _rmsnorm(xp, g_fin).reshape(bp, tp, d)
    y_sample = _rmsnorm(xs, g_fin).reshape(bs, ts, d)

    def stack(states):
        s_new, shift, k, v, c, r, conv = (jnp.stack(z, axis=0) for z in zip(*states))
        return k, v, c, r, s_new, shift, conv

    return (y_prompt, y_sample) + stack(p_states) + stack(s_states)
```

```python
import functools
import math

import numpy as np
import jax
import jax.numpy as jnp
from jax import lax
from jax.experimental import pallas as pl
from jax.experimental.pallas import tpu as pltpu

F32 = jnp.float32
BF16 = jnp.bfloat16
HI = lax.Precision.HIGHEST

RW_HEADS, RW_HD = 8, 64
RW_W = RW_HEADS * RW_HD
W_LORA, A_LORA, G_LORA = 64, 64, 160
RW_COLS = 3 * RW_W + W_LORA + A_LORA + G_LORA
RW_GN_EPS = 64e-5
SB_HEADS, SB_KV, SB_HD = 8, 2, 64
SB_GROUP = SB_HEADS // SB_KV
SB_Q = SB_HEADS * SB_HD
SB_KVW = SB_KV * SB_HD
MLA_HEADS = 8
Q_LORA, KV_LORA, NOPE, ROPE, V_DIM = 256, 256, 64, 32, 64
MLA_SCALE = (NOPE + ROPE) ** -0.5
ROPE_BASE = 10000.0
QC = KV_LORA + ROPE
NORM_EPS = 1e-6
CONV_W = 3

VMEM_LIMIT = 56 * 1024 * 1024
LANES = 128


def _cparams(sem):
    return pltpu.CompilerParams(dimension_semantics=sem, vmem_limit_bytes=VMEM_LIMIT)


def _dot(a, b):
    return jnp.dot(a, b, preferred_element_type=F32)


def _dot_hi(a, b):
    return jnp.dot(a, b, preferred_element_type=F32, precision=HI)


def _dot_nt(a, b):
    return lax.dot_general(a, b, (((1,), (1,)), ((), ())), preferred_element_type=F32)


def _resid(x):
    return x - x.astype(BF16).astype(F32)


def _sigmoid(x):
    return 1.0 / (1.0 + jnp.exp(-x))


def _softplus(x):
    return jnp.maximum(x, 0.0) + jnp.log1p(jnp.exp(-jnp.abs(x)))


def _softplus_pos(x):
    return jnp.maximum(x, 0.0) + jnp.log(1.0 + jnp.exp(-jnp.abs(x)))


def _full(shape):
    n = len(shape)
    return pl.BlockSpec(shape, lambda *_: (0,) * n)


def _row_tile(m, cap=256):
    t = min(cap, m)
    assert m % t == 0
    return t


def _inproj_kernel(x_ref, g_ref, rope_ref, gq_ref, gkv_ref,
                   w_rw, w_sbq, w_sbk, w_sbv, w_cq, w_ckv, w_kr, w_gate,
                   o_rw, o_sbq, o_sbk, o_sbv, o_cq, o_ckv, o_kr, o_gate):
    x = x_ref[...]
    ms = jnp.mean(x * x, axis=-1, keepdims=True)
    h = (x * lax.rsqrt(ms + NORM_EPS) * g_ref[...]).astype(BF16)
    o_rw[...] = _dot(h, w_rw[...])
    o_sbq[...] = _dot(h, w_sbq[...])
    o_sbk[...] = _dot(h, w_sbk[...])
    o_sbv[...] = _dot(h, w_sbv[...])
    o_gate[...] = _dot(h, w_gate[...])

    def lat_norm(c, g):
        return c * lax.rsqrt(jnp.mean(c * c, axis=-1, keepdims=True) + NORM_EPS) * g

    o_cq[...] = lat_norm(_dot(h, w_cq[...]), gq_ref[...])
    o_ckv[...] = lat_norm(_dot(h, w_ckv[...]), gkv_ref[...])
    kr2 = _dot(h, w_kr[...])
    rope = rope_ref[...]
    o_kr[...] = kr2[:, :ROPE] * rope[:, :ROPE] + kr2[:, ROPE:] * rope[:, ROPE:]


def _inproj(x2, g, rope, gq, gkv, ws):
    m, d = x2.shape
    tm = _row_tile(m)
    widths = (RW_COLS, SB_Q, SB_KVW, SB_KVW, Q_LORA, KV_LORA, ROPE, 3 * d)
    row = lambda n: pl.BlockSpec((tm, n), lambda i: (i, 0))
    in_specs = [row(d), _full((1, d)), row(2 * ROPE), _full((1, Q_LORA)), _full((1, KV_LORA))]
    in_specs += [_full(w.shape) for w in ws]
    return pl.pallas_call(
        _inproj_kernel,
        grid=(m // tm,),
        in_specs=in_specs,
        out_specs=[row(n) for n in widths],
        out_shape=[jax.ShapeDtypeStruct((m, n), F32) for n in widths],
        compiler_params=_cparams(("parallel",)),
    )(x2, g, rope, gq, gkv, *ws)


HALO = 8
SEQ_TILE_MIN = 256


def _delayed(x, n, fill):
    out = pltpu.roll(x, n, axis=0)
    rows = lax.broadcasted_iota(jnp.int32, x.shape, 0)
    for r, f in enumerate(fill):
        out = jnp.where(rows == r, f, out)
    return out


def _carry_rows(halo_ref, first_ref, n):
    is_first = pl.program_id(1) == 0
    nf = first_ref.shape[0]
    return [jnp.where(is_first, first_ref[nf - n + r:nf - n + r + 1, :], halo_ref[HALO - n + r:HALO - n + r + 1, :])
            for r in range(n)]


def _rwkv_pre_seq_kernel(c_ref, halo_ref, first_ref, *rest):
    c = c_ref[...]
    _rwkv_pre_body(c, _delayed(c, 1, _carry_rows(halo_ref, first_ref, 1)), *rest)


def _rwkv_pre_kernel(c_ref, p_ref, *rest):
    _rwkv_pre_body(c_ref[...], p_ref[...], *rest)


def _rwkv_pre_body(c, prev, mu_ref, w0_ref, wup_ref, a0_ref, aup_ref, gup_ref,
                   kk_ref, ka_ref, rk_ref, seg_ref,
                   o_r, o_w, o_k, o_v, o_kkn, o_b, o_g, o_bonus):
    xs = c + (prev - c) * mu_ref[...]
    r = xs[:, 0:RW_W]
    k = xs[:, RW_W:2 * RW_W]
    v = xs[:, 2 * RW_W:3 * RW_W]
    o0 = 3 * RW_W
    wd = xs[:, o0:o0 + W_LORA]
    ad = xs[:, o0 + W_LORA:o0 + W_LORA + A_LORA]
    gd = xs[:, o0 + W_LORA + A_LORA:RW_COLS]
    wl = w0_ref[...] + _dot_hi(jnp.tanh(wd), wup_ref[...])
    w = -_softplus(-wl) - 0.5
    decay = jnp.exp(-jnp.exp(w))
    a = _sigmoid(a0_ref[...] + _dot_hi(ad, aup_ref[...]))
    g = _dot_hi(_sigmoid(gd), gup_ref[...])
    seg = seg_ref[...]
    kk = k * kk_ref[...]
    kk = kk * lax.rsqrt(jnp.maximum(_dot_hi(kk * kk, seg), 1e-24))
    k2 = k * (1.0 + (a - 1.0) * ka_ref[...])
    o_r[...] = r
    o_w[...] = decay
    o_k[...] = k2
    o_v[...] = v
    o_kkn[...] = -kk
    o_b[...] = kk * a
    o_g[...] = g
    o_bonus[...] = _dot_hi(r * k2 * rk_ref[...], seg) * v


def _rwkv_pre(c2, p2, mu, w0, wup, a0, aup, gup, kk, ka, rk, seg):
    m = c2.shape[0]
    tm = _row_tile(m)
    rowc = pl.BlockSpec((tm, RW_COLS), lambda i: (i, 0))
    roww = pl.BlockSpec((tm, RW_W), lambda i: (i, 0))
    vec = _full((1, RW_W))
    return pl.pallas_call(
        _rwkv_pre_kernel,
        grid=(m // tm,),
        in_specs=[rowc, rowc, _full((1, RW_COLS)), vec, _full(wup.shape), vec, _full(aup.shape),
                  _full(gup.shape), vec, vec, vec, _full(seg.shape)],
        out_specs=[roww] * 8,
        out_shape=[jax.ShapeDtypeStruct((m, RW_W), F32)] * 8,
        compiler_params=_cparams(("parallel",)),
    )(c2, p2, mu, w0, wup, a0, aup, gup, kk, ka, rk, seg)


def _seq_specs(tm, width):
    tile = pl.BlockSpec((None, tm, width), lambda b, i: (b, i, 0))
    halo = pl.BlockSpec((None, HALO, width), lambda b, i: (b, jnp.maximum(i * (tm // HALO) - 1, 0), 0))
    return tile, halo


def _rwkv_pre_seq(c3, first, mu, w0, wup, a0, aup, gup, kk, ka, rk, seg):
    bsz, t, _ = c3.shape
    tm = _row_tile(t)
    tile, halo = _seq_specs(tm, RW_COLS)
    out = pl.BlockSpec((None, tm, RW_W), lambda b, i: (b, i, 0))
    vec = _full((1, RW_W))
    return pl.pallas_call(
        _rwkv_pre_seq_kernel,
        grid=(bsz, t // tm),
        in_specs=[tile, halo, pl.BlockSpec((None, 1, RW_COLS), lambda b, i: (b, 0, 0)),
                  _full((1, RW_COLS)), vec, _full(wup.shape), vec, _full(aup.shape),
                  _full(gup.shape), vec, vec, vec, _full(seg.shape)],
        out_specs=[out] * 8,
        out_shape=[jax.ShapeDtypeStruct((bsz, t, RW_W), F32)] * 8,
        compiler_params=_cparams(("parallel", "parallel")),
    )(c3, c3, first, mu, w0, wup, a0, aup, gup, kk, ka, rk, seg)


RW_PAIRS = RW_HEADS // 2
RW_SUB = 64
RW_SPLIT = 1


def _rwkv_scan_kernel(bb, tc, r_ref, w_ref, k_ref, v_ref, kkn_ref, b_ref, s0_ref, eye_ref,
                      sum1h_ref, sum2_ref, o_ref, sfin_ref, s_scr, o_scr):
    ci = pl.program_id(1)
    groups = [(bi, hp) for bi in range(bb) for hp in range(RW_PAIRS)]
    per = len(groups) // RW_SPLIT

    @pl.when(ci == 0)
    def _():
        o_scr[...] = jnp.zeros_like(o_scr)
        for gi, (bi, hp) in enumerate(groups):
            s_scr[gi] = jnp.concatenate([s0_ref[bi, 2 * hp], s0_ref[bi, 2 * hp + 1]], axis=1)

    eye2 = eye_ref[...]
    sum1h = sum1h_ref[...]
    sum2 = sum2_ref[...]
    lane = lax.broadcasted_iota(jnp.int32, (RW_HD, LANES), 1) % RW_HD
    sub = min(RW_SUB, tc)
    rows = lambda j: slice(j * RW_HD, (j + 1) * RW_HD)

    def step8(t0, t8, _):
        base = pl.multiple_of(t0 + t8 * 8, 8)
        for part in range(RW_SPLIT):
            gs = list(range(part * per, (part + 1) * per))
            tiles = {}
            for gi in gs:
                bi, hp = groups[gi]
                ls = pl.ds(hp * LANES, LANES)
                tiles[gi] = tuple(ref[bi, pl.ds(base, 8), ls]
                                  for ref in (kkn_ref, v_ref, w_ref, b_ref, k_ref, r_ref))

            def emit_o(ob, step):
                for j, gi in enumerate(gs):
                    o_scr[gi] = jnp.where(lane == t8 * 8 + step, ob[rows(j)], o_scr[gi])

            t2 = None
            vb2 = None
            for i in range(9):
                row = lambda x, i=i: x[i:i + 1, :]
                if i < 8 and i % 2 == 0:
                    nxt = lambda x, i=i: x[i + 1:i + 2, :]
                    vb2 = _dot(jnp.concatenate(
                        [jnp.concatenate([eye2 * row(tiles[gi][1]), eye2 * nxt(tiles[gi][1])], axis=1)
                         for gi in gs], axis=0), sum2)
                if i == 0:
                    sa = _dot(jnp.concatenate([s_scr[gi] * row(tiles[gi][0]) for gi in gs], axis=0), sum1h)
                elif i < 8:
                    both = _dot(jnp.concatenate(
                        [jnp.concatenate([s_scr[gi] * row(tiles[gi][0]), t2[j]], axis=1)
                         for j, gi in enumerate(gs)], axis=0), sum2)
                    sa = both[:, :LANES]
                    emit_o(both[:, LANES:], i - 1)
                else:
                    emit_o(_dot(jnp.concatenate(t2, axis=0), sum1h), 7)
                    break
                vb = vb2[:, :LANES] if i % 2 == 0 else vb2[:, LANES:]
                t2 = []
                for j, gi in enumerate(gs):
                    _, _, w8, b8, k8, r8 = tiles[gi]
                    s = s_scr[gi] * row(w8) + sa[rows(j)] * row(b8) + vb[rows(j)] * row(k8)
                    s_scr[gi] = s
                    t2.append(s * row(r8))
        return 0

    def chunk(c, _):
        t0 = pl.multiple_of(c * sub, sub)
        lax.fori_loop(0, sub // 8, functools.partial(step8, t0), 0)
        for gi, (bi, hp) in enumerate(groups):
            ot = o_scr[gi].T
            o_ref[bi, pl.ds(t0, sub), hp * LANES:hp * LANES + RW_HD] = ot[0:sub]
            o_ref[bi, pl.ds(t0, sub), hp * LANES + RW_HD:(hp + 1) * LANES] = ot[RW_HD:RW_HD + sub]
        return 0

    lax.fori_loop(0, tc // sub, chunk, 0)

    @pl.when(ci == pl.num_programs(1) - 1)
    def _():
        for gi, (bi, hp) in enumerate(groups):
            s = s_scr[gi]
            sfin_ref[bi, 2 * hp] = s[:, :RW_HD]
            sfin_ref[bi, 2 * hp + 1] = s[:, RW_HD:]


def _rwkv_scan(r, w, k, v, kkn, b, s0, consts):
    bsz, t, _ = r.shape
    bb = 4
    tc = min(t, 256)
    assert bsz % bb == 0 and t % tc == 0 and (tc % RW_SUB == 0 or tc < RW_SUB) and tc % 8 == 0
    seq = pl.BlockSpec((bb, tc, RW_W), lambda i, c: (i, c, 0))
    st = pl.BlockSpec((bb, RW_HEADS, RW_HD, RW_HD), lambda i, c: (i, 0, 0, 0))
    return pl.pallas_call(
        functools.partial(_rwkv_scan_kernel, bb, tc),
        grid=(bsz // bb, t // tc),
        in_specs=[seq] * 6 + [st] + [_full(c.shape) for c in consts],
        out_specs=[seq, st],
        out_shape=[jax.ShapeDtypeStruct((bsz, t, RW_W), F32),
                   jax.ShapeDtypeStruct((bsz, RW_HEADS, RW_HD, RW_HD), F32)],
        scratch_shapes=[pltpu.VMEM((bb * RW_PAIRS, RW_HD, LANES), F32),
                        pltpu.VMEM((bb * RW_PAIRS, RW_HD, LANES), F32)],
        compiler_params=_cparams(("parallel", "arbitrary")),
    )(r, w, k, v, kkn, b, s0, *consts)


def _rwkv_post_kernel(o_ref, bonus_ref, g_ref, lnw_ref, lnb_ref, seg_ref, out_ref):
    o = o_ref[...]
    seg = seg_ref[...]
    mean = _dot_hi(o, seg) * (1.0 / RW_HD)
    d = o - mean
    var = _dot_hi(d * d, seg) * (1.0 / RW_HD)
    y = d * lax.rsqrt(var + RW_GN_EPS) * lnw_ref[...] + lnb_ref[...]
    out_ref[...] = (y + bonus_ref[...]) * g_ref[...]


def _rwkv_post(o2, bonus, g, lnw, lnb, seg):
    m = o2.shape[0]
    tm = _row_tile(m)
    roww = pl.BlockSpec((tm, RW_W), lambda i: (i, 0))
    vec = _full((1, RW_W))
    return pl.pallas_call(
        _rwkv_post_kernel,
        grid=(m // tm,),
        in_specs=[roww, roww, roww, vec, vec, _full(seg.shape)],
        out_specs=roww,
        out_shape=jax.ShapeDtypeStruct((m, RW_W), F32),
        compiler_params=_cparams(("parallel",)),
    )(o2, bonus, g, lnw, lnb, seg)


SB_KB = 128


def _sb_block(q, kb, vb, tri, run, acc, mask):
    z = _dot_nt(q, kb.astype(BF16))
    l1m = -_softplus_pos(z)
    if mask is not None:
        l1m = jnp.where(mask, l1m, 0.0)
    after = _dot(jnp.concatenate([l1m, _resid(l1m)], axis=1), tri)
    a = jnp.exp(z + l1m + after + run)
    if mask is not None:
        a = jnp.where(mask, a, 0.0)
    acc = acc + _dot(a.astype(BF16), vb.astype(BF16))
    run = run + after[:, 0:1] + l1m[:, 0:1]
    return run, acc


def _sb_prompt_kernel(tq, q_ref, k_ref, v_ref, tri_ref, o_ref, q_scr, run_scr, acc_scr):
    qi = pl.program_id(1)
    j = pl.program_id(2)
    rows = SB_GROUP * tq
    nsub = tq // SB_KB

    @pl.when(j == 0)
    def _():
        for kvh in range(SB_KV):
            for g in range(SB_GROUP):
                h = kvh * SB_GROUP + g
                q_scr[kvh, g * tq:(g + 1) * tq, :] = (q_ref[:, h * SB_HD:(h + 1) * SB_HD] * SB_HD ** -0.5).astype(BF16)
        run_scr[...] = jnp.zeros_like(run_scr)
        acc_scr[...] = jnp.zeros_like(acc_scr)

    def sweep(diag):
        tri = tri_ref[...]
        for kvh in range(SB_KV):
            q = q_scr[kvh]
            run = run_scr[kvh]
            acc = acc_scr[kvh]
            for sb in reversed(range(nsub)):
                kb = k_ref[sb * SB_KB:(sb + 1) * SB_KB, kvh * SB_HD:(kvh + 1) * SB_HD]
                vb = v_ref[sb * SB_KB:(sb + 1) * SB_KB, kvh * SB_HD:(kvh + 1) * SB_HD]
                mask = None
                if diag:
                    qpos = lax.broadcasted_iota(jnp.int32, (rows, SB_KB), 0) % tq
                    kpos = lax.broadcasted_iota(jnp.int32, (rows, SB_KB), 1) + sb * SB_KB
                    mask = kpos < qpos
                run, acc = _sb_block(q, kb, vb, tri, run, acc, mask)
            run_scr[kvh] = run
            acc_scr[kvh] = acc

    @pl.when(j == 0)
    def _():
        sweep(True)

    @pl.when(jnp.logical_and(j > 0, j <= qi))
    def _():
        sweep(False)

    @pl.when(j == qi)
    def _():
        for kvh in range(SB_KV):
            for g in range(SB_GROUP):
                h = kvh * SB_GROUP + g
                o_ref[:, h * SB_HD:(h + 1) * SB_HD] = acc_scr[kvh, g * tq:(g + 1) * tq, :]


def _sb_prompt(q, k, v, tri):
    bsz, t, _ = q.shape
    tq = min(256, t)
    nq = t // tq
    rows = SB_GROUP * tq
    kv_spec = pl.BlockSpec((None, tq, SB_KVW), lambda b, i, j: (b, jnp.maximum(i - j, 0), 0))
    return pl.pallas_call(
        functools.partial(_sb_prompt_kernel, tq),
        grid=(bsz, nq, nq),
        in_specs=[pl.BlockSpec((None, tq, SB_Q), lambda b, i, j: (b, i, 0)), kv_spec, kv_spec,
                  _full(tri.shape)],
        out_specs=pl.BlockSpec((None, tq, SB_Q), lambda b, i, j: (b, i, 0)),
        out_shape=jax.ShapeDtypeStruct((bsz, t, SB_Q), F32),
        scratch_shapes=[pltpu.VMEM((SB_KV, rows, SB_HD), BF16),
                        pltpu.VMEM((SB_KV, rows, 1), F32),
                        pltpu.VMEM((SB_KV, rows, SB_HD), F32)],
        compiler_params=_cparams(("parallel", "parallel", "arbitrary")),
    )(q, k, v, tri)


N_PAGES_STEP = 16


def _pad_keys(x, n):
    return jnp.concatenate([x, jnp.zeros((n - x.shape[0], x.shape[1]), x.dtype)], axis=0)


def _sb_sample_kernel(t_new, n_pg, *refs):
    q_ref, kn_ref, vn_ref, tri_ref = refs[1:5]
    kt_refs = refs[5:5 + n_pg]
    vt_refs = refs[5 + n_pg:5 + 2 * n_pg]
    o_ref, run_scr, acc_scr = refs[5 + 2 * n_pg:]
    j = pl.program_id(1)
    rows = q_ref.shape[0]
    tri2 = tri_ref[...]
    q = q_ref[...]

    def later_sums(ls):
        lhs = jnp.concatenate([jnp.concatenate([l, _resid(l)], axis=1) for l in ls], axis=0)
        after = _dot(lhs, tri2)
        return [after[i * rows:(i + 1) * rows] for i in range(len(ls))]

    @pl.when(j == 0)
    def _():
        kn = _pad_keys(kn_ref[...], SB_KB)
        vn = _pad_keys(vn_ref[...], SB_KB)
        qpos = lax.broadcasted_iota(jnp.int32, (rows, SB_KB), 0) % t_new
        kpos = lax.broadcasted_iota(jnp.int32, (rows, SB_KB), 1)
        mask = kpos < qpos
        z = _dot_nt(q, kn)
        l1m = jnp.where(mask, -_softplus(z), 0.0)
        after = later_sums([l1m])[0]
        a = jnp.where(mask, jnp.exp(z + l1m + after), 0.0)
        acc_scr[...] = _dot(a, vn)
        run_scr[...] = after[:, 0:1] + l1m[:, 0:1]

    zs = [_dot(q, kt_refs[p][...]) for p in range(n_pg)]
    ls = [-_softplus(z) for z in zs]
    afters = later_sums(ls)
    run = run_scr[...]
    acc = acc_scr[...]
    for p in range(n_pg):
        a = jnp.exp(zs[p] + ls[p] + afters[p] + run)
        acc = acc + _dot_nt(a, vt_refs[p][...])
        run = run + afters[p][:, 0:1] + ls[p][:, 0:1]
    run_scr[...] = run
    acc_scr[...] = acc

    @pl.when(j == pl.num_programs(1) - 1)
    def _():
        o_ref[...] = acc


def _sb_sample(layer, page_table, qpad, kn, vn, cache_kt, cache_vt, tri2, t_new):
    bsz, rows, _ = qpad.shape
    n_pages = page_table.shape[1]
    n_pg = min(N_PAGES_STEP, n_pages)
    assert n_pages % n_pg == 0 and cache_kt.shape[3] == SB_KB
    steps = n_pages // n_pg

    def page_spec(p):
        return pl.BlockSpec((None, None, SB_KVW, SB_KB),
                            lambda b, j, pt: (layer, pt[b, n_pages - 1 - (j * n_pg + p)], 0, 0))

    per_b = lambda n: pl.BlockSpec((None, n, LANES), lambda b, j, pt: (b, 0, 0))
    grid_spec = pltpu.PrefetchScalarGridSpec(
        num_scalar_prefetch=1,
        grid=(bsz, steps),
        in_specs=[per_b(rows), per_b(t_new), per_b(t_new), pl.BlockSpec(tri2.shape, lambda b, j, pt: (0, 0))]
        + [page_spec(p) for p in range(n_pg)] * 2,
        out_specs=per_b(rows),
        scratch_shapes=[pltpu.VMEM((rows, 1), F32), pltpu.VMEM((rows, LANES), F32)],
    )
    return pl.pallas_call(
        functools.partial(_sb_sample_kernel, t_new, n_pg),
        grid_spec=grid_spec,
        out_shape=jax.ShapeDtypeStruct((bsz, rows, LANES), F32),
        compiler_params=_cparams(("parallel", "arbitrary")),
    )(page_table, qpad, kn, vn, tri2, *([cache_kt] * n_pg), *([cache_vt] * n_pg))


def _mla_qprep_kernel(cq_ref, rope_ref, wqn_ref, wqr_ref, wukt_ref, o_ref):
    cq = cq_ref[...].astype(BF16)
    rope = rope_ref[...]
    for h in range(MLA_HEADS):
        qn = _dot(cq, wqn_ref[h])
        qr2 = _dot(cq, wqr_ref[h])
        o_ref[h, :, 0:KV_LORA] = _dot(qn.astype(BF16), wukt_ref[h]) * MLA_SCALE
        o_ref[h, :, KV_LORA:QC] = (qr2[:, :ROPE] * rope[:, :ROPE] + qr2[:, ROPE:] * rope[:, ROPE:]) * MLA_SCALE


def _mla_qprep(cq, rope, wqn, wqr, wukt):
    m = cq.shape[0]
    tm = _row_tile(m)
    return pl.pallas_call(
        _mla_qprep_kernel,
        grid=(m // tm,),
        in_specs=[pl.BlockSpec((tm, Q_LORA), lambda i: (i, 0)), pl.BlockSpec((tm, 2 * ROPE), lambda i: (i, 0)),
                  _full(wqn.shape), _full(wqr.shape), _full(wukt.shape)],
        out_specs=pl.BlockSpec((MLA_HEADS, tm, QC), lambda i: (0, i, 0)),
        out_shape=jax.ShapeDtypeStruct((MLA_HEADS, m, QC), F32),
        compiler_params=_cparams(("parallel",)),
    )(cq, rope, wqn, wqr, wukt)


def _mla_prompt_kernel(tq, q_ref, ckv_ref, kr_ref, o_ref, ql_scr, qr_scr, m_scr, l_scr, acc_scr):
    qi = pl.program_id(1)
    j = pl.program_id(2)

    @pl.when(j == 0)
    def _():
        for h in range(MLA_HEADS):
            ql_scr[h] = q_ref[h, :, 0:KV_LORA].astype(BF16)
            qr_scr[h] = q_ref[h, :, KV_LORA:QC].astype(BF16)
        m_scr[...] = jnp.full_like(m_scr, -jnp.inf)
        l_scr[...] = jnp.zeros_like(l_scr)
        acc_scr[...] = jnp.zeros_like(acc_scr)

    def sweep(diag):
        c32 = ckv_ref[...]
        cb = c32.astype(BF16)
        cbt = c32.T.astype(BF16)
        rb = kr_ref[...].astype(BF16)
        if diag:
            mask = (lax.broadcasted_iota(jnp.int32, (tq, tq), 0) <= lax.broadcasted_iota(jnp.int32, (tq, tq), 1))
        for h in range(MLA_HEADS):
            s = _dot_nt(cb, ql_scr[h]) + _dot_nt(rb, qr_scr[h])
            if diag:
                s = jnp.where(mask, s, -jnp.inf)
            m_prev = m_scr[h]
            m_new = jnp.maximum(m_prev, jnp.max(s, axis=0, keepdims=True))
            alpha = jnp.exp(m_prev - m_new)
            p = jnp.exp(s - m_new)
            l_scr[h] = alpha * l_scr[h] + jnp.sum(p, axis=0, keepdims=True)
            acc_scr[h] = alpha * acc_scr[h] + _dot(cbt, p.astype(BF16))
            m_scr[h] = m_new

    @pl.when(j == 0)
    def _():
        sweep(True)

    @pl.when(jnp.logical_and(j > 0, j <= qi))
    def _():
        sweep(False)

    @pl.when(j == qi)
    def _():
        for h in range(MLA_HEADS):
            o_ref[h] = (acc_scr[h] * (1.0 / l_scr[h])).T


def _mla_prompt(qcat, ckv, kr, bsz, t):
    tq = min(256, t)
    nq = t // tq
    kblk = lambda b, i, j: (b * nq + jnp.maximum(i - j, 0), 0)
    return pl.pallas_call(
        functools.partial(_mla_prompt_kernel, tq),
        grid=(bsz, nq, nq),
        in_specs=[pl.BlockSpec((MLA_HEADS, tq, QC), lambda b, i, j: (0, b * nq + i, 0)),
                  pl.BlockSpec((tq, KV_LORA), kblk), pl.BlockSpec((tq, ROPE), kblk)],
        out_specs=pl.BlockSpec((MLA_HEADS, tq, KV_LORA), lambda b, i, j: (0, b * nq + i, 0)),
        out_shape=jax.ShapeDtypeStruct((MLA_HEADS, bsz * t, KV_LORA), F32),
        scratch_shapes=[pltpu.VMEM((MLA_HEADS, tq, KV_LORA), BF16), pltpu.VMEM((MLA_HEADS, tq, ROPE), BF16),
                        pltpu.VMEM((MLA_HEADS, 1, tq), F32), pltpu.VMEM((MLA_HEADS, 1, tq), F32),
                        pltpu.VMEM((MLA_HEADS, KV_LORA, tq), F32)],
        compiler_params=_cparams(("parallel", "parallel", "arbitrary")),
    )(qcat, ckv, kr)


def _mla_sample_kernel(t_new, n_pg, *refs):
    q_ref, cn_ref, rn_ref = refs[1:4]
    c_refs = refs[4:4 + n_pg]
    rt_refs = refs[4 + n_pg:4 + 2 * n_pg]
    o_ref, m_scr, l_scr, acc_scr = refs[4 + 2 * n_pg:]
    j = pl.program_id(1)
    rows = MLA_HEADS * t_new
    q = q_ref[...].reshape(rows, QC)
    ql = q[:, 0:KV_LORA]
    qr = q[:, KV_LORA:QC]

    @pl.when(j == 0)
    def _():
        cn = _pad_keys(cn_ref[...], SB_KB)
        rn = _pad_keys(rn_ref[...], SB_KB)
        qpos = lax.broadcasted_iota(jnp.int32, (rows, SB_KB), 0) % t_new
        kpos = lax.broadcasted_iota(jnp.int32, (rows, SB_KB), 1)
        s = jnp.where(kpos <= qpos, _dot_nt(ql, cn) + _dot_nt(qr, rn), -jnp.inf)
        m0 = jnp.max(s, axis=-1, keepdims=True)
        p = jnp.exp(s - m0)
        m_scr[...] = m0
        l_scr[...] = jnp.sum(p, axis=-1, keepdims=True)
        acc_scr[...] = _dot(p, cn)

    s = jnp.concatenate([_dot_nt(ql, c_refs[p][...]) + _dot(qr, rt_refs[p][...]) for p in range(n_pg)], axis=1)
    m_prev = m_scr[...]
    m_new = jnp.maximum(m_prev, jnp.max(s, axis=-1, keepdims=True))
    alpha = jnp.exp(m_prev - m_new)
    pr = jnp.exp(s - m_new)
    l_new = alpha * l_scr[...] + jnp.sum(pr, axis=-1, keepdims=True)
    acc = alpha * acc_scr[...]
    page = c_refs[0].shape[0]
    for p in range(n_pg):
        acc = acc + _dot(pr[:, p * page:(p + 1) * page], c_refs[p][...])
    m_scr[...] = m_new
    l_scr[...] = l_new
    acc_scr[...] = acc

    @pl.when(j == pl.num_programs(1) - 1)
    def _():
        o_ref[...] = (acc * (1.0 / l_new)).reshape(MLA_HEADS, t_new, KV_LORA)


def _mla_sample(layer, page_table, qcat, cn, rn, cache_c, cache_rt, t_new):
    bsz = page_table.shape[0]
    n_pages = page_table.shape[1]
    n_pg = min(N_PAGES_STEP, n_pages)
    assert n_pages % n_pg == 0
    steps = n_pages // n_pg
    page = cache_c.shape[2]
    rows = MLA_HEADS * t_new
    pidx = lambda p: (lambda b, j, pt: (layer, pt[b, j * n_pg + p], 0, 0))
    grid_spec = pltpu.PrefetchScalarGridSpec(
        num_scalar_prefetch=1,
        grid=(bsz, steps),
        in_specs=[pl.BlockSpec((MLA_HEADS, t_new, QC), lambda b, j, pt: (0, b, 0)),
                  pl.BlockSpec((None, t_new, KV_LORA), lambda b, j, pt: (b, 0, 0)),
                  pl.BlockSpec((None, t_new, ROPE), lambda b, j, pt: (b, 0, 0))]
        + [pl.BlockSpec((None, None, page, KV_LORA), pidx(p)) for p in range(n_pg)]
        + [pl.BlockSpec((None, None, ROPE, page), pidx(p)) for p in range(n_pg)],
        out_specs=pl.BlockSpec((MLA_HEADS, t_new, KV_LORA), lambda b, j, pt: (0, b, 0)),
        scratch_shapes=[pltpu.VMEM((rows, 1), F32), pltpu.VMEM((rows, 1), F32),
                        pltpu.VMEM((rows, KV_LORA), F32)],
    )
    return pl.pallas_call(
        functools.partial(_mla_sample_kernel, t_new, n_pg),
        grid_spec=grid_spec,
        out_shape=jax.ShapeDtypeStruct((MLA_HEADS, bsz * t_new, KV_LORA), F32),
        compiler_params=_cparams(("parallel", "arbitrary")),
    )(page_table, qcat, cn, rn, *([cache_c] * n_pg), *([cache_rt] * n_pg))


def _merge_kernel(x_ref, gate_ref, orw_ref, osb_ref, olat_ref, wuv_ref, prw_ref, psb_ref, pmla_ref,
                  wout_ref, y_ref):
    d = x_ref.shape[1]
    gate = gate_ref[...]
    o_mla = jnp.concatenate(
        [_dot(olat_ref[h].astype(BF16), wuv_ref[h]) for h in range(MLA_HEADS)], axis=1)
    merged = (_sigmoid(gate[:, 0:d]) * _dot(orw_ref[...].astype(BF16), prw_ref[...])
              + _sigmoid(gate[:, d:2 * d]) * _dot(osb_ref[...].astype(BF16), psb_ref[...])
              + _sigmoid(gate[:, 2 * d:3 * d]) * _dot(o_mla.astype(BF16), pmla_ref[...]))
    y_ref[...] = x_ref[...] + _dot(merged.astype(BF16), wout_ref[...])


def _merge(x2, gate, o_rw, o_sb, o_lat, wuv, prw, psb, pmla, wout):
    m, d = x2.shape
    tm = _row_tile(m)
    row = lambda n: pl.BlockSpec((tm, n), lambda i: (i, 0))
    return pl.pallas_call(
        _merge_kernel,
        grid=(m // tm,),
        in_specs=[row(d), row(3 * d), row(RW_W), row(SB_Q),
                  pl.BlockSpec((MLA_HEADS, tm, KV_LORA), lambda i: (0, i, 0)),
                  _full(wuv.shape), _full(prw.shape), _full(psb.shape), _full(pmla.shape), _full(wout.shape)],
        out_specs=row(d),
        out_shape=jax.ShapeDtypeStruct((m, d), F32),
        compiler_params=_cparams(("parallel",)),
    )(x2, gate, o_rw, o_sb, o_lat, wuv, prw, psb, pmla, wout)


def _ffn_up_kernel(x_ref, g_ref, wa_ref, wb_ref, u_ref, gin_ref):
    x = x_ref[...]
    ms = jnp.mean(x * x, axis=-1, keepdims=True)
    h = (x * lax.rsqrt(ms + NORM_EPS) * g_ref[...]).astype(BF16)
    u_ref[...] = _dot(h, wa_ref[...])
    gin_ref[...] = _dot(h, wb_ref[...])


def _ffn_up(x2, g, wa, wb):
    m, d = x2.shape
    f = wa.shape[1]
    tm = _row_tile(m)
    row = lambda n: pl.BlockSpec((tm, n), lambda i: (i, 0))
    return pl.pallas_call(
        _ffn_up_kernel,
        grid=(m // tm,),
        in_specs=[row(d), _full((1, d)), _full(wa.shape), _full(wb.shape)],
        out_specs=[row(f), row(f)],
        out_shape=[jax.ShapeDtypeStruct((m, f), F32)] * 2,
        compiler_params=_cparams(("parallel",)),
    )(x2, g, wa, wb)


def _gelu_tanh(x):
    c = math.sqrt(2.0 / math.pi)
    return x * (0.5 * (1.0 + jnp.tanh(c * (x + 0.044715 * (x * x * x)))))


def _ffn_down_body(x, u, u1, u2, gin_ref, cw_ref, cb_ref, wd_ref, y_ref):
    cw = cw_ref[...]
    conv = cb_ref[...] + cw[0:1] * u2 + cw[1:2] * u1 + cw[2:3] * u
    act = (_gelu_tanh(conv) * gin_ref[...]).astype(BF16)
    y_ref[...] = x + _dot(act, wd_ref[...])


def _ffn_down_kernel(x_ref, u_ref, u1_ref, u2_ref, *rest):
    _ffn_down_body(x_ref[...], u_ref[...], u1_ref[...], u2_ref[...], *rest)


def _ffn_down_seq_kernel(x_ref, u_ref, halo_ref, first_ref, *rest):
    u = u_ref[...]
    carry = _carry_rows(halo_ref, first_ref, CONV_W - 1)
    _ffn_down_body(x_ref[...], u, _delayed(u, 1, carry[1:]), _delayed(u, 2, carry), *rest)


def _ffn_down_seq(x3, u3, conv_prev, gin3, cw, cb, wd):
    bsz, t, d = x3.shape
    f = u3.shape[2]
    tm = _row_tile(t)
    tile, halo = _seq_specs(tm, f)
    xrow = pl.BlockSpec((None, tm, d), lambda b, i: (b, i, 0))
    return pl.pallas_call(
        _ffn_down_seq_kernel,
        grid=(bsz, t // tm),
        in_specs=[xrow, tile, halo, pl.BlockSpec((None, CONV_W - 1, f), lambda b, i: (b, 0, 0)), tile,
                  _full(cw.shape), _full((1, f)), _full(wd.shape)],
        out_specs=xrow,
        out_shape=jax.ShapeDtypeStruct((bsz, t, d), F32),
        compiler_params=_cparams(("parallel", "parallel")),
    )(x3, u3, u3, conv_prev, gin3, cw, cb, wd)


def _ffn_down(x2, u, u1, u2, gin, cw, cb, wd):
    m, d = x2.shape
    f = u.shape[1]
    tm = _row_tile(m)
    row = lambda n: pl.BlockSpec((tm, n), lambda i: (i, 0))
    return pl.pallas_call(
        _ffn_down_kernel,
        grid=(m // tm,),
        in_specs=[row(d), row(f), row(f), row(f), row(f), _full(cw.shape), _full((1, f)), _full(wd.shape)],
        out_specs=row(d),
        out_shape=jax.ShapeDtypeStruct((m, d), F32),
        compiler_params=_cparams(("parallel",)),
    )(x2, u, u1, u2, gin, cw, cb, wd)


def _rmsnorm_kernel(x_ref, g_ref, y_ref):
    x = x_ref[...]
    y_ref[...] = x * lax.rsqrt(jnp.mean(x * x, axis=-1, keepdims=True) + NORM_EPS) * g_ref[...]


def _rmsnorm(x2, g):
    m, d = x2.shape
    tm = _row_tile(m, 512)
    return pl.pallas_call(
        _rmsnorm_kernel,
        grid=(m // tm,),
        in_specs=[pl.BlockSpec((tm, d), lambda i: (i, 0)), _full((1, d))],
        out_specs=pl.BlockSpec((tm, d), lambda i: (i, 0)),
        out_shape=jax.ShapeDtypeStruct((m, d), F32),
        compiler_params=_cparams(("parallel",)),
    )(x2, g)


def _constants():
    i2 = np.arange(2 * LANES)
    lanes = np.arange(LANES)
    head = lambda x: (x % LANES) // RW_HD
    i5 = np.arange(RW_W)
    kb = np.arange(SB_KB)
    tri = kb[:, None] > kb[None, :]
    sum1 = head(i2)[:, None] == head(lanes)[None, :]
    return dict(
        seg=jnp.asarray((i5[:, None] // RW_HD) == (i5[None, :] // RW_HD), F32),
        scan=(jnp.asarray(np.concatenate([np.eye(RW_HD), np.eye(RW_HD)], axis=1), F32),
              jnp.asarray(sum1[:LANES], F32),
              jnp.asarray((i2[:, None] // RW_HD) == (i2[None, :] // RW_HD), F32)),
        tri2=jnp.asarray(np.concatenate([tri, tri], axis=0), F32),
    )


def _rope_table(pos):
    half = ROPE // 2
    inv = jnp.exp(-math.log(ROPE_BASE) * jnp.arange(half, dtype=F32) / half)
    ang = pos.astype(F32)[:, None] * inv[None, :]
    cos, sin = jnp.cos(ang), jnp.sin(ang)
    return jnp.concatenate([cos, cos, -sin, sin], axis=1)


def _rot_cols(w):
    half = ROPE // 2
    return jnp.concatenate([w[..., half:], w[..., :half]], axis=-1)


def _layer_weights(l, d, p):
    w_in = p['w_in'][l]
    offs = np.cumsum([0, RW_COLS, SB_Q, SB_KVW, SB_KVW, Q_LORA, KV_LORA, ROPE, 3 * d])
    parts = [w_in[:, offs[i]:offs[i + 1]] for i in range(8)]
    parts[6] = jnp.concatenate([parts[6], _rot_cols(parts[6])], axis=1)
    w_uq = p['mla_w_uq'][l]
    wqr = w_uq[:, :, NOPE:]
    row = lambda v: v.reshape(1, -1)
    return dict(
        norm_g=row(p['norm_mix_g'][l]),
        inproj=[w.astype(BF16) for w in parts],
        gq=row(p['mla_q_norm_g'][l]), gkv=row(p['mla_kv_norm_g'][l]),
        mu=row(p['rw_mu'][l]), w0=row(p['rw_w0'][l]), wup=p['rw_w_up'][l], a0=row(p['rw_a0'][l]),
        aup=p['rw_a_up'][l], gup=p['rw_g_up'][l], kk=row(p['rw_k_k'][l]), ka=row(p['rw_k_a'][l]),
        rk=row(p['rw_r_k'][l]), lnw=row(p['rw_ln_w'][l]), lnb=row(p['rw_ln_b'][l]),
        wqn=jnp.transpose(w_uq[:, :, :NOPE], (1, 0, 2)).astype(BF16),
        wqr=jnp.transpose(jnp.concatenate([wqr, _rot_cols(wqr)], axis=-1), (1, 0, 2)).astype(BF16),
        wukt=jnp.transpose(p['mla_w_uk'][l], (1, 2, 0)).astype(BF16),
        wuv=jnp.transpose(p['mla_w_uv'][l], (1, 0, 2)).astype(BF16),
        prw=p['p_rwkv'][l].astype(BF16), psb=p['p_sb'][l].astype(BF16), pmla=p['p_mla'][l].astype(BF16),
        wout=p['w_out'][l].astype(BF16),
        ffn_g=row(p['norm_ffn_g'][l]), wa=p['ffn_w_a'][l].astype(BF16), wb=p['ffn_w_b'][l].astype(BF16),
        cw=p['ffn_conv_w'][l], cb=row(p['ffn_conv_b'][l]), wd=p['ffn_w_down'][l].astype(BF16),
    )


def _shift_rows(x3, first, n):
    assert x3.shape[1] > n
    return jnp.concatenate([first, x3[:, :x3.shape[1] - n]], axis=1)


def _layer(x2, bsz, t, rope, w, cst, rw_s0, rw_shift0, conv_prev, past):
    m, d = x2.shape
    c_rw, sbq, sbk, sbv, cq, ckv, kr, gate = _inproj(x2, w['norm_g'], rope, w['gq'], w['gkv'], w['inproj'])

    c3 = c_rw.reshape(bsz, t, RW_COLS)
    seq_tiled = t >= SEQ_TILE_MIN
    pre_w = (w['mu'], w['w0'], w['wup'], w['a0'], w['aup'], w['gup'], w['kk'], w['ka'], w['rk'], cst['seg'])
    if seq_tiled:
        pre = _rwkv_pre_seq(c3, rw_shift0[:, None, :], *pre_w)
    else:
        prev = _shift_rows(c3, rw_shift0[:, None, :], 1).reshape(m, RW_COLS)
        pre = _rwkv_pre(c_rw, prev, *pre_w)
    r, dec, k2, v, kkn, b, g, bonus = (a.reshape(bsz, t, RW_W) for a in pre)
    o, s_new = _rwkv_scan(r, dec, k2, v, kkn, b, rw_s0, cst['scan'])
    flat = lambda a: a.reshape(m, RW_W)
    o_rw = _rwkv_post(flat(o), flat(bonus), flat(g), w['lnw'], w['lnb'], cst['seg'])

    qcat = _mla_qprep(cq, rope, w['wqn'], w['wqr'], w['wukt'])
    if past is None:
        o_sb = _sb_prompt(sbq.reshape(bsz, t, SB_Q), sbk.reshape(bsz, t, SB_KVW), sbv.reshape(bsz, t, SB_KVW),
                          cst['tri2']).reshape(m, SB_Q)
        o_lat = _mla_prompt(qcat, ckv, kr, bsz, t)
    else:
        layer, page_table, cache_kt, cache_vt, cache_c, cache_rt = past
        q5 = sbq.reshape(bsz, t, SB_KV, SB_GROUP, SB_HD) * SB_HD ** -0.5
        q5 = jnp.transpose(q5, (0, 2, 3, 1, 4)).reshape(bsz, SB_KV, SB_GROUP * t, SB_HD)
        z = jnp.zeros_like(q5[:, 0])
        qpad = jnp.concatenate([jnp.concatenate([q5[:, 0], z], axis=-1),
                                jnp.concatenate([z, q5[:, 1]], axis=-1)], axis=1)
        o_pad = _sb_sample(layer, page_table, qpad, sbk.reshape(bsz, t, SB_KVW), sbv.reshape(bsz, t, SB_KVW),
                           cache_kt, cache_vt, cst['tri2'], t)
        rows = SB_GROUP * t
        o4 = jnp.stack([o_pad[:, :rows, :SB_HD], o_pad[:, rows:, SB_HD:]], axis=1)
        o_sb = jnp.transpose(o4.reshape(bsz, SB_KV, SB_GROUP, t, SB_HD), (0, 3, 1, 2, 4)).reshape(m, SB_Q)
        o_lat = _mla_sample(layer, page_table, qcat, ckv.reshape(bsz, t, KV_LORA), kr.reshape(bsz, t, ROPE),
                            cache_c, cache_rt, t)

    y = _merge(x2, gate, o_rw, o_sb, o_lat, w['wuv'], w['prw'], w['psb'], w['pmla'], w['wout'])

    u, gin = _ffn_up(y, w['ffn_g'], w['wa'], w['wb'])
    f = u.shape[1]
    u3 = u.reshape(bsz, t, f)
    if seq_tiled:
        y = _ffn_down_seq(y.reshape(bsz, t, d), u3, conv_prev, gin.reshape(bsz, t, f),
                          w['cw'], w['cb'], w['wd']).reshape(m, d)
    else:
        u1 = _shift_rows(u3, conv_prev[:, 1:], 1).reshape(m, f)
        u2 = _shift_rows(u3, conv_prev, 2).reshape(m, f)
        y = _ffn_down(y, u, u1, u2, gin, w['cw'], w['cb'], w['wd'])
    conv_new = jnp.concatenate([conv_prev, u3], axis=1)[:, t:]
    states = (s_new, c3[:, -1], sbk.reshape(bsz, t, SB_KV, SB_HD), sbv.reshape(bsz, t, SB_KV, SB_HD),
              ckv.reshape(bsz, t, KV_LORA), kr.reshape(bsz, t, ROPE), conv_new)
    return y, states


def kernel(x_prompt, x_sample, cache_sb_k, cache_sb_v, cache_mla_ckv, cache_mla_krope, state_rwkv, state_rwkv_shift, state_ffn_conv, page_table, norm_mix_g, w_in, rw_mu, rw_w0, rw_w_up, rw_a0, rw_a_up, rw_g_up, rw_k_k, rw_k_a, rw_r_k, rw_ln_w, rw_ln_b, mla_q_norm_g, mla_kv_norm_g, mla_w_uq, mla_w_uk, mla_w_uv, p_rwkv, p_sb, p_mla, w_out, norm_ffn_g, ffn_w_a, ffn_w_b, ffn_conv_w, ffn_conv_b, ffn_w_down, norm_final_g):
    params = dict(norm_mix_g=norm_mix_g, w_in=w_in, rw_mu=rw_mu, rw_w0=rw_w0, rw_w_up=rw_w_up, rw_a0=rw_a0,
                  rw_a_up=rw_a_up, rw_g_up=rw_g_up, rw_k_k=rw_k_k, rw_k_a=rw_k_a, rw_r_k=rw_r_k,
                  rw_ln_w=rw_ln_w, rw_ln_b=rw_ln_b, mla_q_norm_g=mla_q_norm_g, mla_kv_norm_g=mla_kv_norm_g,
                  mla_w_uq=mla_w_uq, mla_w_uk=mla_w_uk, mla_w_uv=mla_w_uv, p_rwkv=p_rwkv, p_sb=p_sb,
                  p_mla=p_mla, w_out=w_out, norm_ffn_g=norm_ffn_g, ffn_w_a=ffn_w_a, ffn_w_b=ffn_w_b,
                  ffn_conv_w=ffn_conv_w, ffn_conv_b=ffn_conv_b, ffn_w_down=ffn_w_down)
    bp, tp, d = x_prompt.shape
    bs, ts, _ = x_sample.shape
    depth = w_in.shape[0]
    n_pool, page = cache_sb_k.shape[1], cache_sb_k.shape[2]
    past_len = page_table.shape[1] * page
    d_ff = ffn_w_a.shape[2]
    assert page == SB_KB and SB_KVW == LANES and ts <= SB_KB

    cst = _constants()
    rope_p = jnp.tile(_rope_table(jnp.arange(tp, dtype=jnp.int32)), (bp, 1))
    rope_s = jnp.tile(_rope_table(past_len + jnp.arange(ts, dtype=jnp.int32)), (bs, 1))
    cache_kt = jnp.transpose(cache_sb_k, (0, 1, 3, 4, 2)).reshape(depth, n_pool, SB_KVW, page)
    cache_vt = jnp.transpose(cache_sb_v, (0, 1, 3, 4, 2)).reshape(depth, n_pool, SB_KVW, page)
    cache_rt = jnp.transpose(cache_mla_krope, (0, 1, 3, 2))
    s0_p = jnp.zeros((bp, RW_HEADS, RW_HD, RW_HD), F32)
    shift0_p = jnp.zeros((bp, RW_COLS), F32)
    conv0_p = jnp.zeros((bp, CONV_W - 1, d_ff), F32)

    xp = x_prompt.reshape(bp * tp, d)
    xs = x_sample.reshape(bs * ts, d)
    p_states, s_states = [], []
    for l in range(depth):
        w = _layer_weights(l, d, params)
        xp, st = _layer(xp, bp, tp, rope_p, w, cst, s0_p, shift0_p, conv0_p, None)
        p_states.append(st)
        past = (l, page_table, cache_kt, cache_vt, cache_mla_ckv, cache_rt)
        xs, st = _layer(xs, bs, ts, rope_s, w, cst, state_rwkv[l], state_rwkv_shift[l], state_ffn_conv[l], past)
        s_states.append(st)

    g_fin = norm_final_g.reshape(1, d)
    y_prompt = _rmsnorm(xp, g_fin).reshape(bp, tp, d)
    y_sample = _rmsnorm(xs, g_fin).reshape(bs, ts, d)

    def stack(states):
        s_new, shift, k, v, c, r, conv = (jnp.stack(z, axis=0) for z in zip(*states))
        return k, v, c, r, s_new, shift, conv

    return (y_prompt, y_sample) + stack(p_states) + stack(s_states)
```

```python
import functools
import math

import numpy as np
import jax
import jax.numpy as jnp
from jax import lax
from jax.experimental import pallas as pl
from jax.experimental.pallas import tpu as pltpu

F32 = jnp.float32
BF16 = jnp.bfloat16
HI = lax.Precision.HIGHEST

RW_HEADS, RW_HD = 8, 64
RW_W = RW_HEADS * RW_HD
W_LORA, A_LORA, G_LORA = 64, 64, 160
RW_COLS = 3 * RW_W + W_LORA + A_LORA + G_LORA
RW_GN_EPS = 64e-5
SB_HEADS, SB_KV, SB_HD = 8, 2, 64
SB_GROUP = SB_HEADS // SB_KV
SB_Q = SB_HEADS * SB_HD
SB_KVW = SB_KV * SB_HD
MLA_HEADS = 8
Q_LORA, KV_LORA, NOPE, ROPE, V_DIM = 256, 256, 64, 32, 64
MLA_SCALE = (NOPE + ROPE) ** -0.5
ROPE_BASE = 10000.0
QC = KV_LORA + ROPE
NORM_EPS = 1e-6
CONV_W = 3

VMEM_LIMIT = 56 * 1024 * 1024
LANES = 128


def _cparams(sem):
    return pltpu.CompilerParams(dimension_semantics=sem, vmem_limit_bytes=VMEM_LIMIT)


def _dot(a, b):
    return jnp.dot(a, b, preferred_element_type=F32)


def _dot_hi(a, b):
    return jnp.dot(a, b, preferred_element_type=F32, precision=HI)


def _dot_nt(a, b):
    return lax.dot_general(a, b, (((1,), (1,)), ((), ())), preferred_element_type=F32)


def _resid(x):
    return x - x.astype(BF16).astype(F32)


def _sigmoid(x):
    return 1.0 / (1.0 + jnp.exp(-x))


def _softplus(x):
    return jnp.maximum(x, 0.0) + jnp.log1p(jnp.exp(-jnp.abs(x)))


def _softplus_pos(x):
    return jnp.maximum(x, 0.0) + jnp.log(1.0 + jnp.exp(-jnp.abs(x)))


def _full(shape):
    n = len(shape)
    return pl.BlockSpec(shape, lambda *_: (0,) * n)


def _row_tile(m, cap=256):
    t = min(cap, m)
    assert m % t == 0
    return t


def _inproj_kernel(x_ref, g_ref, rope_ref, gq_ref, gkv_ref,
                   w_rw, w_sbq, w_sbk, w_sbv, w_cq, w_ckv, w_kr, w_gate,
                   o_rw, o_sbq, o_sbk, o_sbv, o_cq, o_ckv, o_kr, o_gate):
    x = x_ref[...]
    ms = jnp.mean(x * x, axis=-1, keepdims=True)
    h = (x * lax.rsqrt(ms + NORM_EPS) * g_ref[...]).astype(BF16)
    o_rw[...] = _dot(h, w_rw[...])
    o_sbq[...] = _dot(h, w_sbq[...])
    o_sbk[...] = _dot(h, w_sbk[...])
    o_sbv[...] = _dot(h, w_sbv[...])
    o_gate[...] = _dot(h, w_gate[...])

    def lat_norm(c, g):
        return c * lax.rsqrt(jnp.mean(c * c, axis=-1, keepdims=True) + NORM_EPS) * g

    o_cq[...] = lat_norm(_dot(h, w_cq[...]), gq_ref[...])
    o_ckv[...] = lat_norm(_dot(h, w_ckv[...]), gkv_ref[...])
    kr2 = _dot(h, w_kr[...])
    rope = rope_ref[...]
    o_kr[...] = kr2[:, :ROPE] * rope[:, :ROPE] + kr2[:, ROPE:] * rope[:, ROPE:]


def _inproj(x2, g, rope, gq, gkv, ws):
    m, d = x2.shape
    tm = _row_tile(m)
    widths = (RW_COLS, SB_Q, SB_KVW, SB_KVW, Q_LORA, KV_LORA, ROPE, 3 * d)
    row = lambda n: pl.BlockSpec((tm, n), lambda i: (i, 0))
    in_specs = [row(d), _full((1, d)), row(2 * ROPE), _full((1, Q_LORA)), _full((1, KV_LORA))]
    in_specs += [_full(w.shape) for w in ws]
    return pl.pallas_call(
        _inproj_kernel,
        grid=(m // tm,),
        in_specs=in_specs,
        out_specs=[row(n) for n in widths],
        out_shape=[jax.ShapeDtypeStruct((m, n), F32) for n in widths],
        compiler_params=_cparams(("parallel",)),
    )(x2, g, rope, gq, gkv, *ws)


HALO = 8
SEQ_TILE_MIN = 256


def _delayed(x, n, fill):
    out = pltpu.roll(x, n, axis=0)
    rows = lax.broadcasted_iota(jnp.int32, x.shape, 0)
    for r, f in enumerate(fill):
        out = jnp.where(rows == r, f, out)
    return out


def _carry_rows(halo_ref, first_ref, n):
    is_first = pl.program_id(1) == 0
    nf = first_ref.shape[0]
    return [jnp.where(is_first, first_ref[nf - n + r:nf - n + r + 1, :], halo_ref[HALO - n + r:HALO - n + r + 1, :])
            for r in range(n)]


def _rwkv_pre_seq_kernel(c_ref, halo_ref, first_ref, *rest):
    c = c_ref[...]
    _rwkv_pre_body(c, _delayed(c, 1, _carry_rows(halo_ref, first_ref, 1)), *rest)


def _rwkv_pre_kernel(c_ref, p_ref, *rest):
    _rwkv_pre_body(c_ref[...], p_ref[...], *rest)


def _rwkv_pre_body(c, prev, mu_ref, w0_ref, wup_ref, a0_ref, aup_ref, gup_ref,
                   kk_ref, ka_ref, rk_ref, seg_ref,
                   o_r, o_w, o_k, o_v, o_kkn, o_b, o_g, o_bonus):
    xs = c + (prev - c) * mu_ref[...]
    r = xs[:, 0:RW_W]
    k = xs[:, RW_W:2 * RW_W]
    v = xs[:, 2 * RW_W:3 * RW_W]
    o0 = 3 * RW_W
    wd = xs[:, o0:o0 + W_LORA]
    ad = xs[:, o0 + W_LORA:o0 + W_LORA + A_LORA]
    gd = xs[:, o0 + W_LORA + A_LORA:RW_COLS]
    wl = w0_ref[...] + _dot_hi(jnp.tanh(wd), wup_ref[...])
    w = -_softplus(-wl) - 0.5
    decay = jnp.exp(-jnp.exp(w))
    a = _sigmoid(a0_ref[...] + _dot_hi(ad, aup_ref[...]))
    g = _dot_hi(_sigmoid(gd), gup_ref[...])
    seg = seg_ref[...]
    kk = k * kk_ref[...]
    kk = kk * lax.rsqrt(jnp.maximum(_dot_hi(kk * kk, seg), 1e-24))
    k2 = k * (1.0 + (a - 1.0) * ka_ref[...])
    o_r[...] = r
    o_w[...] = decay
    o_k[...] = k2
    o_v[...] = v
    o_kkn[...] = -kk
    o_b[...] = kk * a
    o_g[...] = g
    o_bonus[...] = _dot_hi(r * k2 * rk_ref[...], seg) * v


def _rwkv_pre(c2, p2, mu, w0, wup, a0, aup, gup, kk, ka, rk, seg):
    m = c2.shape[0]
    tm = _row_tile(m)
    rowc = pl.BlockSpec((tm, RW_COLS), lambda i: (i, 0))
    roww = pl.BlockSpec((tm, RW_W), lambda i: (i, 0))
    vec = _full((1, RW_W))
    return pl.pallas_call(
        _rwkv_pre_kernel,
        grid=(m // tm,),
        in_specs=[rowc, rowc, _full((1, RW_COLS)), vec, _full(wup.shape), vec, _full(aup.shape),
                  _full(gup.shape), vec, vec, vec, _full(seg.shape)],
        out_specs=[roww] * 8,
        out_shape=[jax.ShapeDtypeStruct((m, RW_W), F32)] * 8,
        compiler_params=_cparams(("parallel",)),
    )(c2, p2, mu, w0, wup, a0, aup, gup, kk, ka, rk, seg)


def _seq_specs(tm, width):
    tile = pl.BlockSpec((None, tm, width), lambda b, i: (b, i, 0))
    halo = pl.BlockSpec((None, HALO, width), lambda b, i: (b, jnp.maximum(i * (tm // HALO) - 1, 0), 0))
    return tile, halo


def _rwkv_pre_seq(c3, first, mu, w0, wup, a0, aup, gup, kk, ka, rk, seg):
    bsz, t, _ = c3.shape
    tm = _row_tile(t)
    tile, halo = _seq_specs(tm, RW_COLS)
    out = pl.BlockSpec((None, tm, RW_W), lambda b, i: (b, i, 0))
    vec = _full((1, RW_W))
    return pl.pallas_call(
        _rwkv_pre_seq_kernel,
        grid=(bsz, t // tm),
        in_specs=[tile, halo, pl.BlockSpec((None, 1, RW_COLS), lambda b, i: (b, 0, 0)),
                  _full((1, RW_COLS)), vec, _full(wup.shape), vec, _full(aup.shape),
                  _full(gup.shape), vec, vec, vec, _full(seg.shape)],
        out_specs=[out] * 8,
        out_shape=[jax.ShapeDtypeStruct((bsz, t, RW_W), F32)] * 8,
        compiler_params=_cparams(("parallel", "parallel")),
    )(c3, c3, first, mu, w0, wup, a0, aup, gup, kk, ka, rk, seg)


RW_PAIRS = RW_HEADS // 2
RW_SUB = 64
RW_SPLIT = 1


def _rwkv_scan_kernel(bb, tc, r_ref, w_ref, k_ref, v_ref, kkn_ref, b_ref, s0_ref, eye_ref,
                      sum1h_ref, sum2_ref, o_ref, sfin_ref, s_scr, o_scr):
    ci = pl.program_id(1)
    groups = [(bi, hp) for bi in range(bb) for hp in range(RW_PAIRS)]
    per = len(groups) // RW_SPLIT

    @pl.when(ci == 0)
    def _():
        o_scr[...] = jnp.zeros_like(o_scr)
        for gi, (bi, hp) in enumerate(groups):
            s_scr[gi] = jnp.concatenate([s0_ref[bi, 2 * hp], s0_ref[bi, 2 * hp + 1]], axis=1)

    eye2 = eye_ref[...]
    sum1h = sum1h_ref[...]
    sum2 = sum2_ref[...]
    lane = lax.broadcasted_iota(jnp.int32, (RW_HD, LANES), 1) % RW_HD
    sub = min(RW_SUB, tc)
    rows = lambda j: slice(j * RW_HD, (j + 1) * RW_HD)

    def step8(t0, t8, _):
        base = pl.multiple_of(t0 + t8 * 8, 8)
        for part in range(RW_SPLIT):
            gs = list(range(part * per, (part + 1) * per))
            tiles = {}
            for gi in gs:
                bi, hp = groups[gi]
                ls = pl.ds(hp * LANES, LANES)
                tiles[gi] = tuple(ref[bi, pl.ds(base, 8), ls]
                                  for ref in (kkn_ref, v_ref, w_ref, b_ref, k_ref, r_ref))

            def emit_o(ob, step):
                for j, gi in enumerate(gs):
                    o_scr[gi] = jnp.where(lane == t8 * 8 + step, ob[rows(j)], o_scr[gi])

            t2 = None
            vb2 = None
            for i in range(9):
                row = lambda x, i=i: x[i:i + 1, :]
                if i < 8 and i % 2 == 0:
                    nxt = lambda x, i=i: x[i + 1:i + 2, :]
                    vb2 = _dot(jnp.concatenate(
                        [jnp.concatenate([eye2 * row(tiles[gi][1]), eye2 * nxt(tiles[gi][1])], axis=1)
                         for gi in gs], axis=0), sum2)
                if i == 0:
                    sa = _dot(jnp.concatenate([s_scr[gi] * row(tiles[gi][0]) for gi in gs], axis=0), sum1h)
                elif i < 8:
                    both = _dot(jnp.concatenate(
                        [jnp.concatenate([s_scr[gi] * row(tiles[gi][0]), t2[j]], axis=1)
                         for j, gi in enumerate(gs)], axis=0), sum2)
                    sa = both[:, :LANES]
                    emit_o(both[:, LANES:], i - 1)
                else:
                    emit_o(_dot(jnp.concatenate(t2, axis=0), sum1h), 7)
                    break
                vb = vb2[:, :LANES] if i % 2 == 0 else vb2[:, LANES:]
                t2 = []
                for j, gi in enumerate(gs):
                    _, _, w8, b8, k8, r8 = tiles[gi]
                    s = s_scr[gi] * row(w8) + sa[rows(j)] * row(b8) + vb[rows(j)] * row(k8)
                    s_scr[gi] = s
                    t2.append(s * row(r8))
        return 0

    def chunk(c, _):
        t0 = pl.multiple_of(c * sub, sub)
        lax.fori_loop(0, sub // 8, functools.partial(step8, t0), 0)
        for gi, (bi, hp) in enumerate(groups):
            ot = o_scr[gi].T
            o_ref[bi, pl.ds(t0, sub), hp * LANES:hp * LANES + RW_HD] = ot[0:sub]
            o_ref[bi, pl.ds(t0, sub), hp * LANES + RW_HD:(hp + 1) * LANES] = ot[RW_HD:RW_HD + sub]
        return 0

    lax.fori_loop(0, tc // sub, chunk, 0)

    @pl.when(ci == pl.num_programs(1) - 1)
    def _():
        for gi, (bi, hp) in enumerate(groups):
            s = s_scr[gi]
            sfin_ref[bi, 2 * hp] = s[:, :RW_HD]
            sfin_ref[bi, 2 * hp + 1] = s[:, RW_HD:]


def _rwkv_scan(r, w, k, v, kkn, b, s0, consts):
    bsz, t, _ = r.shape
    bb = 4
    tc = min(t, 256)
    assert bsz % bb == 0 and t % tc == 0 and (tc % RW_SUB == 0 or tc < RW_SUB) and tc % 8 == 0
    seq = pl.BlockSpec((bb, tc, RW_W), lambda i, c: (i, c, 0))
    st = pl.BlockSpec((bb, RW_HEADS, RW_HD, RW_HD), lambda i, c: (i, 0, 0, 0))
    return pl.pallas_call(
        functools.partial(_rwkv_scan_kernel, bb, tc),
        grid=(bsz // bb, t // tc),
        in_specs=[seq] * 6 + [st] + [_full(c.shape) for c in consts],
        out_specs=[seq, st],
        out_shape=[jax.ShapeDtypeStruct((bsz, t, RW_W), F32),
                   jax.ShapeDtypeStruct((bsz, RW_HEADS, RW_HD, RW_HD), F32)],
        scratch_shapes=[pltpu.VMEM((bb * RW_PAIRS, RW_HD, LANES), F32),
                        pltpu.VMEM((bb * RW_PAIRS, RW_HD, LANES), F32)],
        compiler_params=_cparams(("parallel", "arbitrary")),
    )(r, w, k, v, kkn, b, s0, *consts)


def _rwkv_post_kernel(o_ref, bonus_ref, g_ref, lnw_ref, lnb_ref, seg_ref, out_ref):
    o = o_ref[...]
    seg = seg_ref[...]
    mean = _dot_hi(o, seg) * (1.0 / RW_HD)
    d = o - mean
    var = _dot_hi(d * d, seg) * (1.0 / RW_HD)
    y = d * lax.rsqrt(var + RW_GN_EPS) * lnw_ref[...] + lnb_ref[...]
    out_ref[...] = (y + bonus_ref[...]) * g_ref[...]


def _rwkv_post(o2, bonus, g, lnw, lnb, seg):
    m = o2.shape[0]
    tm = _row_tile(m)
    roww = pl.BlockSpec((tm, RW_W), lambda i: (i, 0))
    vec = _full((1, RW_W))
    return pl.pallas_call(
        _rwkv_post_kernel,
        grid=(m // tm,),
        in_specs=[roww, roww, roww, vec, vec, _full(seg.shape)],
        out_specs=roww,
        out_shape=jax.ShapeDtypeStruct((m, RW_W), F32),
        compiler_params=_cparams(("parallel",)),
    )(o2, bonus, g, lnw, lnb, seg)


SB_KB = 128


def _sb_block(q, kb, vb, tri, run, acc, mask):
    z = _dot_nt(q, kb.astype(BF16))
    l1m = -_softplus_pos(z)
    if mask is not None:
        l1m = jnp.where(mask, l1m, 0.0)
    after = _dot(jnp.concatenate([l1m, _resid(l1m)], axis=1), tri)
    a = jnp.exp(z + l1m + after + run)
    if mask is not None:
        a = jnp.where(mask, a, 0.0)
    acc = acc + _dot(a.astype(BF16), vb.astype(BF16))
    run = run + after[:, 0:1] + l1m[:, 0:1]
    return run, acc


def _sb_prompt_kernel(tq, q_ref, k_ref, v_ref, tri_ref, o_ref, q_scr, run_scr, acc_scr):
    qi = pl.program_id(1)
    j = pl.program_id(2)
    rows = SB_GROUP * tq
    nsub = tq // SB_KB

    @pl.when(j == 0)
    def _():
        for kvh in range(SB_KV):
            for g in range(SB_GROUP):
                h = kvh * SB_GROUP + g
                q_scr[kvh, g * tq:(g + 1) * tq, :] = (q_ref[:, h * SB_HD:(h + 1) * SB_HD] * SB_HD ** -0.5).astype(BF16)
        run_scr[...] = jnp.zeros_like(run_scr)
        acc_scr[...] = jnp.zeros_like(acc_scr)

    def sweep(diag):
        tri = tri_ref[...]
        for kvh in range(SB_KV):
            q = q_scr[kvh]
            run = run_scr[kvh]
            acc = acc_scr[kvh]
            for sb in reversed(range(nsub)):
                kb = k_ref[sb * SB_KB:(sb + 1) * SB_KB, kvh * SB_HD:(kvh + 1) * SB_HD]
                vb = v_ref[sb * SB_KB:(sb + 1) * SB_KB, kvh * SB_HD:(kvh + 1) * SB_HD]
                mask = None
                if diag:
                    qpos = lax.broadcasted_iota(jnp.int32, (rows, SB_KB), 0) % tq
                    kpos = lax.broadcasted_iota(jnp.int32, (rows, SB_KB), 1) + sb * SB_KB
                    mask = kpos < qpos
                run, acc = _sb_block(q, kb, vb, tri, run, acc, mask)
            run_scr[kvh] = run
            acc_scr[kvh] = acc

    @pl.when(j == 0)
    def _():
        sweep(True)

    @pl.when(jnp.logical_and(j > 0, j <= qi))
    def _():
        sweep(False)

    @pl.when(j == qi)
    def _():
        for kvh in range(SB_KV):
            for g in range(SB_GROUP):
                h = kvh * SB_GROUP + g
                o_ref[:, h * SB_HD:(h + 1) * SB_HD] = acc_scr[kvh, g * tq:(g + 1) * tq, :]


def _sb_prompt(q, k, v, tri):
    bsz, t, _ = q.shape
    tq = min(256, t)
    nq = t // tq
    rows = SB_GROUP * tq
    kv_spec = pl.BlockSpec((None, tq, SB_KVW), lambda b, i, j: (b, jnp.maximum(i - j, 0), 0))
    return pl.pallas_call(
        functools.partial(_sb_prompt_kernel, tq),
        grid=(bsz, nq, nq),
        in_specs=[pl.BlockSpec((None, tq, SB_Q), lambda b, i, j: (b, i, 0)), kv_spec, kv_spec,
                  _full(tri.shape)],
        out_specs=pl.BlockSpec((None, tq, SB_Q), lambda b, i, j: (b, i, 0)),
        out_shape=jax.ShapeDtypeStruct((bsz, t, SB_Q), F32),
        scratch_shapes=[pltpu.VMEM((SB_KV, rows, SB_HD), BF16),
                        pltpu.VMEM((SB_KV, rows, 1), F32),
                        pltpu.VMEM((SB_KV, rows, SB_HD), F32)],
        compiler_params=_cparams(("parallel", "parallel", "arbitrary")),
    )(q, k, v, tri)


N_PAGES_STEP = 32


def _pad_keys(x, n):
    return jnp.concatenate([x, jnp.zeros((n - x.shape[0], x.shape[1]), x.dtype)], axis=0)


def _sb_sample_kernel(t_new, n_pg, *refs):
    q_ref, kn_ref, vn_ref, tri_ref = refs[1:5]
    kt_refs = refs[5:5 + n_pg]
    vt_refs = refs[5 + n_pg:5 + 2 * n_pg]
    o_ref, run_scr, acc_scr = refs[5 + 2 * n_pg:]
    j = pl.program_id(1)
    rows = q_ref.shape[0]
    tri2 = tri_ref[...]
    q = q_ref[...]

    def later_sums(ls):
        lhs = jnp.concatenate([jnp.concatenate([l, _resid(l)], axis=1) for l in ls], axis=0)
        after = _dot(lhs, tri2)
        return [after[i * rows:(i + 1) * rows] for i in range(len(ls))]

    @pl.when(j == 0)
    def _():
        kn = _pad_keys(kn_ref[...], SB_KB)
        vn = _pad_keys(vn_ref[...], SB_KB)
        qpos = lax.broadcasted_iota(jnp.int32, (rows, SB_KB), 0) % t_new
        kpos = lax.broadcasted_iota(jnp.int32, (rows, SB_KB), 1)
        mask = kpos < qpos
        z = _dot_nt(q, kn)
        l1m = jnp.where(mask, -_softplus(z), 0.0)
        after = later_sums([l1m])[0]
        a = jnp.where(mask, jnp.exp(z + l1m + after), 0.0)
        acc_scr[...] = _dot(a, vn)
        run_scr[...] = after[:, 0:1] + l1m[:, 0:1]

    zs = [_dot(q, kt_refs[p][...]) for p in range(n_pg)]
    ls = [-_softplus(z) for z in zs]
    afters = later_sums(ls)
    run = run_scr[...]
    acc = acc_scr[...]
    for p in range(n_pg):
        a = jnp.exp(zs[p] + ls[p] + afters[p] + run)
        acc = acc + _dot_nt(a, vt_refs[p][...])
        run = run + afters[p][:, 0:1] + ls[p][:, 0:1]
    run_scr[...] = run
    acc_scr[...] = acc

    @pl.when(j == pl.num_programs(1) - 1)
    def _():
        o_ref[...] = acc


def _sb_sample(layer, page_table, qpad, kn, vn, cache_kt, cache_vt, tri2, t_new):
    bsz, rows, _ = qpad.shape
    n_pages = page_table.shape[1]
    n_pg = min(N_PAGES_STEP, n_pages)
    assert n_pages % n_pg == 0 and cache_kt.shape[3] == SB_KB
    steps = n_pages // n_pg

    def page_spec(p):
        return pl.BlockSpec((None, None, SB_KVW, SB_KB),
                            lambda b, j, pt: (layer, pt[b, n_pages - 1 - (j * n_pg + p)], 0, 0))

    per_b = lambda n: pl.BlockSpec((None, n, LANES), lambda b, j, pt: (b, 0, 0))
    grid_spec = pltpu.PrefetchScalarGridSpec(
        num_scalar_prefetch=1,
        grid=(bsz, steps),
        in_specs=[per_b(rows), per_b(t_new), per_b(t_new), pl.BlockSpec(tri2.shape, lambda b, j, pt: (0, 0))]
        + [page_spec(p) for p in range(n_pg)] * 2,
        out_specs=per_b(rows),
        scratch_shapes=[pltpu.VMEM((rows, 1), F32), pltpu.VMEM((rows, LANES), F32)],
    )
    return pl.pallas_call(
        functools.partial(_sb_sample_kernel, t_new, n_pg),
        grid_spec=grid_spec,
        out_shape=jax.ShapeDtypeStruct((bsz, rows, LANES), F32),
        compiler_params=_cparams(("parallel", "arbitrary")),
    )(page_table, qpad, kn, vn, tri2, *([cache_kt] * n_pg), *([cache_vt] * n_pg))


def _mla_qprep_kernel(cq_ref, rope_ref, wqn_ref, wqr_ref, wukt_ref, o_ref):
    cq = cq_ref[...].astype(BF16)
    rope = rope_ref[...]
    for h in range(MLA_HEADS):
        qn = _dot(cq, wqn_ref[h])
        qr2 = _dot(cq, wqr_ref[h])
        o_ref[h, :, 0:KV_LORA] = _dot(qn.astype(BF16), wukt_ref[h]) * MLA_SCALE
        o_ref[h, :, KV_LORA:QC] = (qr2[:, :ROPE] * rope[:, :ROPE] + qr2[:, ROPE:] * rope[:, ROPE:]) * MLA_SCALE


def _mla_qprep(cq, rope, wqn, wqr, wukt):
    m = cq.shape[0]
    tm = _row_tile(m)
    return pl.pallas_call(
        _mla_qprep_kernel,
        grid=(m // tm,),
        in_specs=[pl.BlockSpec((tm, Q_LORA), lambda i: (i, 0)), pl.BlockSpec((tm, 2 * ROPE), lambda i: (i, 0)),
                  _full(wqn.shape), _full(wqr.shape), _full(wukt.shape)],
        out_specs=pl.BlockSpec((MLA_HEADS, tm, QC), lambda i: (0, i, 0)),
        out_shape=jax.ShapeDtypeStruct((MLA_HEADS, m, QC), F32),
        compiler_params=_cparams(("parallel",)),
    )(cq, rope, wqn, wqr, wukt)


def _mla_prompt_kernel(tq, tk, q_ref, ckv_ref, kr_ref, o_ref, ql_scr, qr_scr, m_scr, l_scr, acc_scr):
    qi = pl.program_id(1)
    j = pl.program_id(2)
    kq = tk // tq
    last = qi // kq

    @pl.when(j == 0)
    def _():
        for h in range(MLA_HEADS):
            ql_scr[h] = q_ref[h, :, 0:KV_LORA].astype(BF16)
            qr_scr[h] = q_ref[h, :, KV_LORA:QC].astype(BF16)
        m_scr[...] = jnp.full_like(m_scr, -jnp.inf)
        l_scr[...] = jnp.zeros_like(l_scr)
        acc_scr[...] = jnp.zeros_like(acc_scr)

    def sweep(diag):
        c32 = ckv_ref[...]
        cb = c32.astype(BF16)
        cbt = c32.T.astype(BF16)
        rb = kr_ref[...].astype(BF16)
        if diag:
            mask = (lax.broadcasted_iota(jnp.int32, (tk, tq), 0)
                    <= lax.broadcasted_iota(jnp.int32, (tk, tq), 1) + (qi % kq) * tq)
        for h in range(MLA_HEADS):
            s = _dot_nt(cb, ql_scr[h]) + _dot_nt(rb, qr_scr[h])
            if diag:
                s = jnp.where(mask, s, -jnp.inf)
            m_prev = m_scr[h]
            m_new = jnp.maximum(m_prev, jnp.max(s, axis=0, keepdims=True))
            alpha = jnp.exp(m_prev - m_new)
            p = jnp.exp(s - m_new)
            l_scr[h] = alpha * l_scr[h] + jnp.sum(p, axis=0, keepdims=True)
            acc_scr[h] = alpha * acc_scr[h] + _dot(cbt, p.astype(BF16))
            m_scr[h] = m_new

    @pl.when(j == 0)
    def _():
        sweep(True)

    @pl.when(jnp.logical_and(j > 0, j <= last))
    def _():
        sweep(False)

    @pl.when(j == last)
    def _():
        for h in range(MLA_HEADS):
            o_ref[h] = (acc_scr[h] * (1.0 / l_scr[h])).T


def _mla_prompt(qcat, ckv, kr, bsz, t):
    tq = min(256, t)
    tk = 2 * tq if t % (2 * tq) == 0 else tq
    nq, nk, kq = t // tq, t // tk, tk // tq
    kblk = lambda b, i, j: (b * nk + jnp.maximum(i // kq - j, 0), 0)
    return pl.pallas_call(
        functools.partial(_mla_prompt_kernel, tq, tk),
        grid=(bsz, nq, nk),
        in_specs=[pl.BlockSpec((MLA_HEADS, tq, QC), lambda b, i, j: (0, b * nq + i, 0)),
                  pl.BlockSpec((tk, KV_LORA), kblk), pl.BlockSpec((tk, ROPE), kblk)],
        out_specs=pl.BlockSpec((MLA_HEADS, tq, KV_LORA), lambda b, i, j: (0, b * nq + i, 0)),
        out_shape=jax.ShapeDtypeStruct((MLA_HEADS, bsz * t, KV_LORA), F32),
        scratch_shapes=[pltpu.VMEM((MLA_HEADS, tq, KV_LORA), BF16), pltpu.VMEM((MLA_HEADS, tq, ROPE), BF16),
                        pltpu.VMEM((MLA_HEADS, 1, tq), F32), pltpu.VMEM((MLA_HEADS, 1, tq), F32),
                        pltpu.VMEM((MLA_HEADS, KV_LORA, tq), F32)],
        compiler_params=_cparams(("parallel", "parallel", "arbitrary")),
    )(qcat, ckv, kr)


def _mla_sample_kernel(t_new, n_pg, *refs):
    q_ref, cn_ref, rn_ref = refs[1:4]
    c_refs = refs[4:4 + n_pg]
    rt_refs = refs[4 + n_pg:4 + 2 * n_pg]
    o_ref, m_scr, l_scr, acc_scr = refs[4 + 2 * n_pg:]
    j = pl.program_id(1)
    rows = MLA_HEADS * t_new
    q = q_ref[...].reshape(rows, QC)
    ql = q[:, 0:KV_LORA]
    qr = q[:, KV_LORA:QC]

    @pl.when(j == 0)
    def _():
        cn = _pad_keys(cn_ref[...], SB_KB)
        rn = _pad_keys(rn_ref[...], SB_KB)
        qpos = lax.broadcasted_iota(jnp.int32, (rows, SB_KB), 0) % t_new
        kpos = lax.broadcasted_iota(jnp.int32, (rows, SB_KB), 1)
        s = jnp.where(kpos <= qpos, _dot_nt(ql, cn) + _dot_nt(qr, rn), -jnp.inf)
        m0 = jnp.max(s, axis=-1, keepdims=True)
        p = jnp.exp(s - m0)
        m_scr[...] = m0
        l_scr[...] = jnp.sum(p, axis=-1, keepdims=True)
        acc_scr[...] = _dot(p, cn)

    s = jnp.concatenate([_dot_nt(ql, c_refs[p][...]) + _dot(qr, rt_refs[p][...]) for p in range(n_pg)], axis=1)
    m_prev = m_scr[...]
    m_new = jnp.maximum(m_prev, jnp.max(s, axis=-1, keepdims=True))
    alpha = jnp.exp(m_prev - m_new)
    pr = jnp.exp(s - m_new)
    l_new = alpha * l_scr[...] + jnp.sum(pr, axis=-1, keepdims=True)
    acc = alpha * acc_scr[...]
    page = c_refs[0].shape[0]
    for p in range(n_pg):
        acc = acc + _dot(pr[:, p * page:(p + 1) * page], c_refs[p][...])
    m_scr[...] = m_new
    l_scr[...] = l_new
    acc_scr[...] = acc

    @pl.when(j == pl.num_programs(1) - 1)
    def _():
        o_ref[...] = (acc * (1.0 / l_new)).reshape(MLA_HEADS, t_new, KV_LORA)


def _mla_sample(layer, page_table, qcat, cn, rn, cache_c, cache_rt, t_new):
    bsz = page_table.shape[0]
    n_pages = page_table.shape[1]
    n_pg = min(N_PAGES_STEP, n_pages)
    assert n_pages % n_pg == 0
    steps = n_pages // n_pg
    page = cache_c.shape[2]
    rows = MLA_HEADS * t_new
    pidx = lambda p: (lambda b, j, pt: (layer, pt[b, j * n_pg + p], 0, 0))
    grid_spec = pltpu.PrefetchScalarGridSpec(
        num_scalar_prefetch=1,
        grid=(bsz, steps),
        in_specs=[pl.BlockSpec((MLA_HEADS, t_new, QC), lambda b, j, pt: (0, b, 0)),
                  pl.BlockSpec((None, t_new, KV_LORA), lambda b, j, pt: (b, 0, 0)),
                  pl.BlockSpec((None, t_new, ROPE), lambda b, j, pt: (b, 0, 0))]
        + [pl.BlockSpec((None, None, page, KV_LORA), pidx(p)) for p in range(n_pg)]
        + [pl.BlockSpec((None, None, ROPE, page), pidx(p)) for p in range(n_pg)],
        out_specs=pl.BlockSpec((MLA_HEADS, t_new, KV_LORA), lambda b, j, pt: (0, b, 0)),
        scratch_shapes=[pltpu.VMEM((rows, 1), F32), pltpu.VMEM((rows, 1), F32),
                        pltpu.VMEM((rows, KV_LORA), F32)],
    )
    return pl.pallas_call(
        functools.partial(_mla_sample_kernel, t_new, n_pg),
        grid_spec=grid_spec,
        out_shape=jax.ShapeDtypeStruct((MLA_HEADS, bsz * t_new, KV_LORA), F32),
        compiler_params=_cparams(("parallel", "arbitrary")),
    )(page_table, qcat, cn, rn, *([cache_c] * n_pg), *([cache_rt] * n_pg))


def _merge_kernel(x_ref, gate_ref, orw_ref, osb_ref, olat_ref, wuv_ref, prw_ref, psb_ref, pmla_ref,
                  wout_ref, y_ref):
    d = x_ref.shape[1]
    gate = gate_ref[...]
    o_mla = jnp.concatenate(
        [_dot(olat_ref[h].astype(BF16), wuv_ref[h]) for h in range(MLA_HEADS)], axis=1)
    merged = (_sigmoid(gate[:, 0:d]) * _dot(orw_ref[...].astype(BF16), prw_ref[...])
              + _sigmoid(gate[:, d:2 * d]) * _dot(osb_ref[...].astype(BF16), psb_ref[...])
              + _sigmoid(gate[:, 2 * d:3 * d]) * _dot(o_mla.astype(BF16), pmla_ref[...]))
    y_ref[...] = x_ref[...] + _dot(merged.astype(BF16), wout_ref[...])


def _merge(x2, gate, o_rw, o_sb, o_lat, wuv, prw, psb, pmla, wout):
    m, d = x2.shape
    tm = _row_tile(m)
    row = lambda n: pl.BlockSpec((tm, n), lambda i: (i, 0))
    return pl.pallas_call(
        _merge_kernel,
        grid=(m // tm,),
        in_specs=[row(d), row(3 * d), row(RW_W), row(SB_Q),
                  pl.BlockSpec((MLA_HEADS, tm, KV_LORA), lambda i: (0, i, 0)),
                  _full(wuv.shape), _full(prw.shape), _full(psb.shape), _full(pmla.shape), _full(wout.shape)],
        out_specs=row(d),
        out_shape=jax.ShapeDtypeStruct((m, d), F32),
        compiler_params=_cparams(("parallel",)),
    )(x2, gate, o_rw, o_sb, o_lat, wuv, prw, psb, pmla, wout)


def _ffn_up_kernel(x_ref, g_ref, wa_ref, wb_ref, u_ref, gin_ref):
    x = x_ref[...]
    ms = jnp.mean(x * x, axis=-1, keepdims=True)
    h = (x * lax.rsqrt(ms + NORM_EPS) * g_ref[...]).astype(BF16)
    u_ref[...] = _dot(h, wa_ref[...])
    gin_ref[...] = _dot(h, wb_ref[...])


def _ffn_up(x2, g, wa, wb):
    m, d = x2.shape
    f = wa.shape[1]
    tm = _row_tile(m)
    row = lambda n: pl.BlockSpec((tm, n), lambda i: (i, 0))
    return pl.pallas_call(
        _ffn_up_kernel,
        grid=(m // tm,),
        in_specs=[row(d), _full((1, d)), _full(wa.shape), _full(wb.shape)],
        out_specs=[row(f), row(f)],
        out_shape=[jax.ShapeDtypeStruct((m, f), F32)] * 2,
        compiler_params=_cparams(("parallel",)),
    )(x2, g, wa, wb)


def _gelu_tanh(x):
    c = math.sqrt(2.0 / math.pi)
    return x * (0.5 * (1.0 + jnp.tanh(c * (x + 0.044715 * (x * x * x)))))


def _ffn_down_body(x, u, u1, u2, gin_ref, cw_ref, cb_ref, wd_ref, y_ref):
    cw = cw_ref[...]
    conv = cb_ref[...] + cw[0:1] * u2 + cw[1:2] * u1 + cw[2:3] * u
    act = (_gelu_tanh(conv) * gin_ref[...]).astype(BF16)
    y_ref[...] = x + _dot(act, wd_ref[...])


def _ffn_down_kernel(x_ref, u_ref, u1_ref, u2_ref, *rest):
    _ffn_down_body(x_ref[...], u_ref[...], u1_ref[...], u2_ref[...], *rest)


def _ffn_down_seq_kernel(x_ref, u_ref, halo_ref, first_ref, *rest):
    u = u_ref[...]
    carry = _carry_rows(halo_ref, first_ref, CONV_W - 1)
    _ffn_down_body(x_ref[...], u, _delayed(u, 1, carry[1:]), _delayed(u, 2, carry), *rest)


def _ffn_down_seq(x3, u3, conv_prev, gin3, cw, cb, wd):
    bsz, t, d = x3.shape
    f = u3.shape[2]
    tm = _row_tile(t)
    tile, halo = _seq_specs(tm, f)
    xrow = pl.BlockSpec((None, tm, d), lambda b, i: (b, i, 0))
    return pl.pallas_call(
        _ffn_down_seq_kernel,
        grid=(bsz, t // tm),
        in_specs=[xrow, tile, halo, pl.BlockSpec((None, CONV_W - 1, f), lambda b, i: (b, 0, 0)), tile,
                  _full(cw.shape), _full((1, f)), _full(wd.shape)],
        out_specs=xrow,
        out_shape=jax.ShapeDtypeStruct((bsz, t, d), F32),
        compiler_params=_cparams(("parallel", "parallel")),
    )(x3, u3, u3, conv_prev, gin3, cw, cb, wd)


def _ffn_down(x2, u, u1, u2, gin, cw, cb, wd):
    m, d = x2.shape
    f = u.shape[1]
    tm = _row_tile(m)
    row = lambda n: pl.BlockSpec((tm, n), lambda i: (i, 0))
    return pl.pallas_call(
        _ffn_down_kernel,
        grid=(m // tm,),
        in_specs=[row(d), row(f), row(f), row(f), row(f), _full(cw.shape), _full((1, f)), _full(wd.shape)],
        out_specs=row(d),
        out_shape=jax.ShapeDtypeStruct((m, d), F32),
        compiler_params=_cparams(("parallel",)),
    )(x2, u, u1, u2, gin, cw, cb, wd)


def _rmsnorm_kernel(x_ref, g_ref, y_ref):
    x = x_ref[...]
    y_ref[...] = x * lax.rsqrt(jnp.mean(x * x, axis=-1, keepdims=True) + NORM_EPS) * g_ref[...]


def _rmsnorm(x2, g):
    m, d = x2.shape
    tm = _row_tile(m, 512)
    return pl.pallas_call(
        _rmsnorm_kernel,
        grid=(m // tm,),
        in_specs=[pl.BlockSpec((tm, d), lambda i: (i, 0)), _full((1, d))],
        out_specs=pl.BlockSpec((tm, d), lambda i: (i, 0)),
        out_shape=jax.ShapeDtypeStruct((m, d), F32),
        compiler_params=_cparams(("parallel",)),
    )(x2, g)


def _constants():
    i2 = np.arange(2 * LANES)
    lanes = np.arange(LANES)
    head = lambda x: (x % LANES) // RW_HD
    i5 = np.arange(RW_W)
    kb = np.arange(SB_KB)
    tri = kb[:, None] > kb[None, :]
    sum1 = head(i2)[:, None] == head(lanes)[None, :]
    return dict(
        seg=jnp.asarray((i5[:, None] // RW_HD) == (i5[None, :] // RW_HD), F32),
        scan=(jnp.asarray(np.concatenate([np.eye(RW_HD), np.eye(RW_HD)], axis=1), F32),
              jnp.asarray(sum1[:LANES], F32),
              jnp.asarray((i2[:, None] // RW_HD) == (i2[None, :] // RW_HD), F32)),
        tri2=jnp.asarray(np.concatenate([tri, tri], axis=0), F32),
    )


def _rope_table(pos):
    half = ROPE // 2
    inv = jnp.exp(-math.log(ROPE_BASE) * jnp.arange(half, dtype=F32) / half)
    ang = pos.astype(F32)[:, None] * inv[None, :]
    cos, sin = jnp.cos(ang), jnp.sin(ang)
    return jnp.concatenate([cos, cos, -sin, sin], axis=1)


def _rot_cols(w):
    half = ROPE // 2
    return jnp.concatenate([w[..., half:], w[..., :half]], axis=-1)


def _layer_weights(l, d, p):
    w_in = p['w_in'][l]
    offs = np.cumsum([0, RW_COLS, SB_Q, SB_KVW, SB_KVW, Q_LORA, KV_LORA, ROPE, 3 * d])
    parts = [w_in[:, offs[i]:offs[i + 1]] for i in range(8)]
    parts[6] = jnp.concatenate([parts[6], _rot_cols(parts[6])], axis=1)
    w_uq = p['mla_w_uq'][l]
    wqr = w_uq[:, :, NOPE:]
    row = lambda v: v.reshape(1, -1)
    return dict(
        norm_g=row(p['norm_mix_g'][l]),
        inproj=[w.astype(BF16) for w in parts],
        gq=row(p['mla_q_norm_g'][l]), gkv=row(p['mla_kv_norm_g'][l]),
        mu=row(p['rw_mu'][l]), w0=row(p['rw_w0'][l]), wup=p['rw_w_up'][l], a0=row(p['rw_a0'][l]),
        aup=p['rw_a_up'][l], gup=p['rw_g_up'][l], kk=row(p['rw_k_k'][l]), ka=row(p['rw_k_a'][l]),
        rk=row(p['rw_r_k'][l]), lnw=row(p['rw_ln_w'][l]), lnb=row(p['rw_ln_b'][l]),
        wqn=jnp.transpose(w_uq[:, :, :NOPE], (1, 0, 2)).astype(BF16),
        wqr=jnp.transpose(jnp.concatenate([wqr, _rot_cols(wqr)], axis=-1), (1, 0, 2)).astype(BF16),
        wukt=jnp.transpose(p['mla_w_uk'][l], (1, 2, 0)).astype(BF16),
        wuv=jnp.transpose(p['mla_w_uv'][l], (1, 0, 2)).astype(BF16),
        prw=p['p_rwkv'][l].astype(BF16), psb=p['p_sb'][l].astype(BF16), pmla=p['p_mla'][l].astype(BF16),
        wout=p['w_out'][l].astype(BF16),
        ffn_g=row(p['norm_ffn_g'][l]), wa=p['ffn_w_a'][l].astype(BF16), wb=p['ffn_w_b'][l].astype(BF16),
        cw=p['ffn_conv_w'][l], cb=row(p['ffn_conv_b'][l]), wd=p['ffn_w_down'][l].astype(BF16),
    )


def _shift_rows(x3, first, n):
    assert x3.shape[1] > n
    return jnp.concatenate([first, x3[:, :x3.shape[1] - n]], axis=1)


def _layer(x2, bsz, t, rope, w, cst, rw_s0, rw_shift0, conv_prev, past):
    m, d = x2.shape
    c_rw, sbq, sbk, sbv, cq, ckv, kr, gate = _inproj(x2, w['norm_g'], rope, w['gq'], w['gkv'], w['inproj'])

    c3 = c_rw.reshape(bsz, t, RW_COLS)
    seq_tiled = t >= SEQ_TILE_MIN
    pre_w = (w['mu'], w['w0'], w['wup'], w['a0'], w['aup'], w['gup'], w['kk'], w['ka'], w['rk'], cst['seg'])
    if seq_tiled:
        pre = _rwkv_pre_seq(c3, rw_shift0[:, None, :], *pre_w)
    else:
        prev = _shift_rows(c3, rw_shift0[:, None, :], 1).reshape(m, RW_COLS)
        pre = _rwkv_pre(c_rw, prev, *pre_w)
    r, dec, k2, v, kkn, b, g, bonus = (a.reshape(bsz, t, RW_W) for a in pre)
    o, s_new = _rwkv_scan(r, dec, k2, v, kkn, b, rw_s0, cst['scan'])
    flat = lambda a: a.reshape(m, RW_W)
    o_rw = _rwkv_post(flat(o), flat(bonus), flat(g), w['lnw'], w['lnb'], cst['seg'])

    qcat = _mla_qprep(cq, rope, w['wqn'], w['wqr'], w['wukt'])
    if past is None:
        o_sb = _sb_prompt(sbq.reshape(bsz, t, SB_Q), sbk.reshape(bsz, t, SB_KVW), sbv.reshape(bsz, t, SB_KVW),
                          cst['tri2']).reshape(m, SB_Q)
        o_lat = _mla_prompt(qcat, ckv, kr, bsz, t)
    else:
        layer, page_table, cache_kt, cache_vt, cache_c, cache_rt = past
        q5 = sbq.reshape(bsz, t, SB_KV, SB_GROUP, SB_HD) * SB_HD ** -0.5
        q5 = jnp.transpose(q5, (0, 2, 3, 1, 4)).reshape(bsz, SB_KV, SB_GROUP * t, SB_HD)
        z = jnp.zeros_like(q5[:, 0])
        qpad = jnp.concatenate([jnp.concatenate([q5[:, 0], z], axis=-1),
                                jnp.concatenate([z, q5[:, 1]], axis=-1)], axis=1)
        o_pad = _sb_sample(layer, page_table, qpad, sbk.reshape(bsz, t, SB_KVW), sbv.reshape(bsz, t, SB_KVW),
                           cache_kt, cache_vt, cst['tri2'], t)
        rows = SB_GROUP * t
        o4 = jnp.stack([o_pad[:, :rows, :SB_HD], o_pad[:, rows:, SB_HD:]], axis=1)
        o_sb = jnp.transpose(o4.reshape(bsz, SB_KV, SB_GROUP, t, SB_HD), (0, 3, 1, 2, 4)).reshape(m, SB_Q)
        o_lat = _mla_sample(layer, page_table, qcat, ckv.reshape(bsz, t, KV_LORA), kr.reshape(bsz, t, ROPE),
                            cache_c, cache_rt, t)

    y = _merge(x2, gate, o_rw, o_sb, o_lat, w['wuv'], w['prw'], w['psb'], w['pmla'], w['wout'])

    u, gin = _ffn_up(y, w['ffn_g'], w['wa'], w['wb'])
    f = u.shape[1]
    u3 = u.reshape(bsz, t, f)
    if seq_tiled:
        y = _ffn_down_seq(y.reshape(bsz, t, d), u3, conv_prev, gin.reshape(bsz, t, f),
                          w['cw'], w['cb'], w['wd']).reshape(m, d)
    else:
        u1 = _shift_rows(u3, conv_prev[:, 1:], 1).reshape(m, f)
        u2 = _shift_rows(u3, conv_prev, 2).reshape(m, f)
        y = _ffn_down(y, u, u1, u2, gin, w['cw'], w['cb'], w['wd'])
    conv_new = jnp.concatenate([conv_prev, u3], axis=1)[:, t:]
    states = (s_new, c3[:, -1], sbk.reshape(bsz, t, SB_KV, SB_HD), sbv.reshape(bsz, t, SB_KV, SB_HD),
              ckv.reshape(bsz, t, KV_LORA), kr.reshape(bsz, t, ROPE), conv_new)
    return y, states


def kernel(x_prompt, x_sample, cache_sb_k, cache_sb_v, cache_mla_ckv, cache_mla_krope, state_rwkv, state_rwkv_shift, state_ffn_conv, page_table, norm_mix_g, w_in, rw_mu, rw_w0, rw_w_up, rw_a0, rw_a_up, rw_g_up, rw_k_k, rw_k_a, rw_r_k, rw_ln_w, rw_ln_b, mla_q_norm_g, mla_kv_norm_g, mla_w_uq, mla_w_uk, mla_w_uv, p_rwkv, p_sb, p_mla, w_out, norm_ffn_g, ffn_w_a, ffn_w_b, ffn_conv_w, ffn_conv_b, ffn_w_down, norm_final_g):
    params = dict(norm_mix_g=norm_mix_g, w_in=w_in, rw_mu=rw_mu, rw_w0=rw_w0, rw_w_up=rw_w_up, rw_a0=rw_a0,
                  rw_a_up=rw_a_up, rw_g_up=rw_g_up, rw_k_k=rw_k_k, rw_k_a=rw_k_a, rw_r_k=rw_r_k,
                  rw_ln_w=rw_ln_w, rw_ln_b=rw_ln_b, mla_q_norm_g=mla_q_norm_g, mla_kv_norm_g=mla_kv_norm_g,
                  mla_w_uq=mla_w_uq, mla_w_uk=mla_w_uk, mla_w_uv=mla_w_uv, p_rwkv=p_rwkv, p_sb=p_sb,
                  p_mla=p_mla, w_out=w_out, norm_ffn_g=norm_ffn_g, ffn_w_a=ffn_w_a, ffn_w_b=ffn_w_b,
                  ffn_conv_w=ffn_conv_w, ffn_conv_b=ffn_conv_b, ffn_w_down=ffn_w_down)
    bp, tp, d = x_prompt.shape
    bs, ts, _ = x_sample.shape
    depth = w_in.shape[0]
    n_pool, page = cache_sb_k.shape[1], cache_sb_k.shape[2]
    past_len = page_table.shape[1] * page
    d_ff = ffn_w_a.shape[2]
    assert page == SB_KB and SB_KVW == LANES and ts <= SB_KB

    cst = _constants()
    rope_p = jnp.tile(_rope_table(jnp.arange(tp, dtype=jnp.int32)), (bp, 1))
    rope_s = jnp.tile(_rope_table(past_len + jnp.arange(ts, dtype=jnp.int32)), (bs, 1))
    cache_kt = jnp.transpose(cache_sb_k, (0, 1, 3, 4, 2)).reshape(depth, n_pool, SB_KVW, page)
    cache_vt = jnp.transpose(cache_sb_v, (0, 1, 3, 4, 2)).reshape(depth, n_pool, SB_KVW, page)
    cache_rt = jnp.transpose(cache_mla_krope, (0, 1, 3, 2))
    s0_p = jnp.zeros((bp, RW_HEADS, RW_HD, RW_HD), F32)
    shift0_p = jnp.zeros((bp, RW_COLS), F32)
    conv0_p = jnp.zeros((bp, CONV_W - 1, d_ff), F32)

    xp = x_prompt.reshape(bp * tp, d)
    xs = x_sample.reshape(bs * ts, d)
    p_states, s_states = [], []
    for l in range(depth):
        w = _layer_weights(l, d, params)
        xp, st = _layer(xp, bp, tp, rope_p, w, cst, s0_p, shift0_p, conv0_p, None)
        p_states.append(st)
        past = (l, page_table, cache_kt, cache_vt, cache_mla_ckv, cache_rt)
        xs, st = _layer(xs, bs, ts, rope_s, w, cst, state_rwkv[l], state_rwkv_shift[l], state_ffn_conv[l], past)
        s_states.append(st)

    g_fin = norm_final_g.reshape(1, d)
    y_prompt = _rmsnorm(xp, g_fin).reshape(bp, tp, d)
    y_sample = _rmsnorm(xs, g_fin).reshape(bs, ts, d)

    def stack(states):
        s_new, shift, k, v, c, r, conv = (jnp.stack(z, axis=0) for z in zip(*states))
        return k, v, c, r, s_new, shift, conv

    return (y_prompt, y_sample) + stack(p_states) + stack(s_states)
```
